```python
import math
import jax, jax.numpy as jnp
from jax import lax
import numpy as np

D_MODEL = 2048
BATCH = 16
SEQ = 2048
DEPTH = 1

CTX_LEN = 256
GRID_W = 64
DN_HEADS = 8
DN_HEAD_DIM = 128
DN_WIDTH = DN_HEADS * DN_HEAD_DIM
DN_CONV = 3
DN_CHUNK = 64
SC_WIDTH = 1024
SC_CONV = 3
N_EXPERTS = 16
EXPERT_FF = 2048
CAPACITY_FACTOR = 2
NORM_EPS = 1e-6
N_STATE_COLS = 3 * DN_WIDTH + 4 * DN_HEADS
N_IN_COLS = N_STATE_COLS + DN_WIDTH + 3 * SC_WIDTH + 2 * D_MODEL

kernel_name = "hybrid_gdn_shortconv_ec_moe_diffusion"


def rmsnorm(x, w):
    xf = x.astype(jnp.float32)
    y = xf * lax.rsqrt(jnp.mean(xf * xf, axis=-1, keepdims=True) + NORM_EPS)
    return (y * w.astype(jnp.float32)).astype(x.dtype)


def modulate(h, shift, scale):
    return h * (1 + scale) + shift


def l2norm(x):
    return x * lax.rsqrt(jnp.sum(x * x, axis=-1, keepdims=True) + NORM_EPS)


def dwconv_centred(x, w):
    k = w.shape[0]
    p = k // 2
    return lax.conv_general_dilated(
        x, w[:, None, :].astype(x.dtype), window_strides=(1,), padding=[(p, p)],
        dimension_numbers=("NWC", "WIO", "NWC"), feature_group_count=x.shape[-1])


def grid_row_conv(x, w):
    b, t, ch = x.shape
    rows = t // GRID_W
    y = dwconv_centred(x.reshape(b * rows, GRID_W, ch), w)
    return y.reshape(b, t, ch)


def gated_delta_rule(q, k, v, beta, g, s0):
    b, h, t, dk = q.shape
    dv = v.shape[-1]
    L = DN_CHUNK
    n = t // L
    q, k, v = (a.reshape(b, h, n, L, a.shape[-1]) for a in (q, k, v))
    beta = beta.reshape(b, h, n, L)
    gc = jnp.cumsum(g.reshape(b, h, n, L), axis=-1)
    pos = jnp.arange(L)
    incl = pos[:, None] >= pos[None, :]
    strict = pos[:, None] > pos[None, :]
    decay = jnp.exp(jnp.where(incl, gc[..., :, None] - gc[..., None, :], -jnp.inf))
    kb = k * beta[..., None]
    a_low = jnp.where(strict, jnp.einsum("bhnid,bhnjd->bhnij", kb, k) * decay, 0.0)
    eye = jnp.eye(L, dtype=q.dtype)
    rhs = jnp.concatenate([v * beta[..., None], kb * jnp.exp(gc)[..., None]], axis=-1)
    sol = lax.linalg.triangular_solve(a_low + eye, rhs, left_side=True, lower=True,
                                      unit_diagonal=True)
    u, w = sol[..., :dv], sol[..., dv:]
    qk = jnp.einsum("bhnid,bhnjd->bhnij", q, k) * decay
    q_dec = q * jnp.exp(gc)[..., None]
    k_dec = k * jnp.exp(gc[..., -1:] - gc)[..., None]
    g_last = jnp.exp(gc[..., -1])
    xs = tuple(jnp.moveaxis(a, 2, 0) for a in (q_dec, k_dec, u, w, qk, g_last))

    def step(s, inp):
        qd, kd, ui, wi, qki, gl = inp
        v_new = ui - jnp.einsum("bhld,bhdv->bhlv", wi, s)
        o = jnp.einsum("bhld,bhdv->bhlv", qd, s) + jnp.einsum("bhij,bhjv->bhiv", qki, v_new)
        s = s * gl[..., None, None] + jnp.einsum("bhld,bhlv->bhdv", kd, v_new)
        return s, o

    s_final, o = lax.scan(step, s0, xs)
    o = jnp.moveaxis(o, 0, 2).reshape(b, h, t, dv)
    return o, s_final


def deltanet_inputs(p, conv_w, a_log, dt_bias):
    b, t, _ = p.shape
    f32 = jnp.float32
    qkv = jax.nn.silu(dwconv_centred(p[..., :3 * DN_WIDTH], conv_w)).astype(f32)
    heads = lambda a: a.reshape(b, t, DN_HEADS, DN_HEAD_DIM).transpose(0, 2, 1, 3)
    q, k, v = (heads(a) for a in jnp.split(qkv, 3, axis=-1))
    q = l2norm(q) * (DN_HEAD_DIM ** -0.5)
    k = l2norm(k)
    dirs = lambda a: a.astype(f32).reshape(b, t, 2, DN_HEADS).transpose(2, 0, 3, 1)
    o_b = 3 * DN_WIDTH
    beta = jax.nn.sigmoid(dirs(p[..., o_b:o_b + 2 * DN_HEADS]))
    a_raw = dirs(p[..., o_b + 2 * DN_HEADS:N_STATE_COLS])
    g = -jnp.exp(a_log.astype(f32))[:, None, :, None] * jax.nn.softplus(
        a_raw + dt_bias.astype(f32)[:, None, :, None])
    return q, k, v, beta, g


def bidirectional_deltanet(ctx_in, lat_in):
    qc, kc, vc, bc, gc = ctx_in
    ql, kl, vl, bl, gl = lat_in
    flip = lambda a: jnp.flip(a, axis=2)
    b, h, _, dk = qc.shape
    s0 = jnp.zeros((b, h, dk, vc.shape[-1]), jnp.float32)
    oc_f, sc_f = gated_delta_rule(qc, kc, vc, bc[0], gc[0], s0)
    ol_f, _ = gated_delta_rule(ql, kl, vl, bl[0], gl[0], sc_f)
    oc_b, sc_b = gated_delta_rule(flip(qc), flip(kc), flip(vc), flip(bc[1]), flip(gc[1]), s0)
    ol_b, _ = gated_delta_rule(flip(ql), flip(kl), flip(vl), flip(bl[1]), flip(gl[1]), sc_b)
    return oc_f + flip(oc_b), ol_f + flip(ol_b)


def deltanet_output(o, z, gn_w, w_out):
    b, h, t, dv = o.shape
    o = o.transpose(0, 2, 1, 3)
    o = o * lax.rsqrt(jnp.mean(o * o, axis=-1, keepdims=True) + NORM_EPS) * gn_w.astype(jnp.float32)
    o = o * jax.nn.silu(z.astype(jnp.float32)).reshape(b, t, h, dv)
    return o.reshape(b, t, h * dv).astype(z.dtype) @ w_out


def short_conv_branch(p_sc, conv_w, w_out, on_grid):
    bg, cg, xin = jnp.split(p_sc, 3, axis=-1)
    zc = cg * xin
    zc = grid_row_conv(zc, conv_w) if on_grid else dwconv_centred(zc, conv_w)
    return (bg * zc) @ w_out


def merge_branches(p, dn_o, gn_w, w_dn_out, conv_sc, w_sc_out, w_o, on_grid):
    z = p[..., N_STATE_COLS:N_STATE_COLS + DN_WIDTH]
    sc = p[..., N_STATE_COLS + DN_WIDTH:N_STATE_COLS + DN_WIDTH + 3 * SC_WIDTH]
    gates = jax.nn.sigmoid(p[..., N_IN_COLS - 2 * D_MODEL:])
    g_a, g_b = jnp.split(gates, 2, axis=-1)
    y_a = deltanet_output(dn_o, z, gn_w, w_dn_out)
    y_b = short_conv_branch(sc, conv_sc, w_sc_out, on_grid)
    return (g_a * y_a + g_b * y_b) @ w_o


def expert_choice_ffn(h, w_router, w_gate, w_up, w_down):
    b, n, d = h.shape
    cap = CAPACITY_FACTOR * n // N_EXPERTS
    aff = jax.nn.softmax((h @ w_router).astype(jnp.float32), axis=-1)
    top_aff, idx = lax.top_k(aff.transpose(0, 2, 1), cap)
    xs = jax.vmap(lambda hb, ib: hb[ib])(h, idx)
    hid = jax.nn.silu(jnp.einsum("becd,edf->becf", xs, w_gate)) * jnp.einsum("becd,edf->becf", xs, w_up)
    ye = jnp.einsum("becf,efd->becd", hid, w_down) * top_aff[..., None].astype(h.dtype)
    return jax.vmap(lambda yb, ib: jnp.zeros((n, d), yb.dtype).at[ib.reshape(-1)].add(
        yb.reshape(-1, d)))(ye, idx)


def setup_inputs(seed: int = 0) -> dict:
    key = jax.random.key(seed)
    ks = jax.random.split(key, 24)
    D = D_MODEL
    nrm = lambda k, shape, fan_in: jax.random.normal(k, shape, jnp.float32) * (fan_in ** -0.5)
    gain = lambda k, shape: 1.0 + 0.05 * jax.random.normal(k, shape, jnp.float32)
    dt = jnp.exp(jax.random.uniform(ks[10], (DEPTH, 2, DN_HEADS), jnp.float32,
                                    minval=math.log(1e-3), maxval=math.log(1e-1)))
    return {
        "x": jax.random.normal(ks[0], (BATCH, SEQ, D), jnp.float32),
        "c": jax.random.normal(ks[1], (BATCH, D), jnp.float32),
        "ctx": jax.random.normal(ks[2], (BATCH, CTX_LEN, D), jnp.float32),
        "c_ctx": jax.random.normal(ks[3], (D,), jnp.float32),
        "w_ada": nrm(ks[4], (DEPTH, D, 6 * D), D),
        "b_ada": 0.02 * jax.random.normal(ks[5], (DEPTH, 6 * D), jnp.float32),
        "norm1": gain(ks[6], (DEPTH, D)),
        "norm2": gain(ks[7], (DEPTH, D)),
        "w_in": nrm(ks[8], (DEPTH, D, N_IN_COLS), D),
        "conv_qkv": nrm(ks[9], (DEPTH, DN_CONV, 3 * DN_WIDTH), DN_CONV),
        "a_log": jnp.log(jax.random.uniform(ks[11], (DEPTH, 2, DN_HEADS), jnp.float32,
                                            minval=1.0, maxval=16.0)),
        "dt_bias": dt + jnp.log(-jnp.expm1(-dt)),
        "gn_w": gain(ks[12], (DEPTH, DN_HEAD_DIM)),
        "w_dn_out": nrm(ks[13], (DEPTH, DN_WIDTH, D), DN_WIDTH),
        "conv_sc": nrm(ks[14], (DEPTH, SC_CONV, SC_WIDTH), SC_CONV),
        "w_sc_out": nrm(ks[15], (DEPTH, SC_WIDTH, D), SC_WIDTH),
        "w_o": nrm(ks[16], (DEPTH, D, D), D),
        "w_router": nrm(ks[17], (DEPTH, D, N_EXPERTS), D),
        "w_gate": nrm(ks[18], (DEPTH, N_EXPERTS, D, EXPERT_FF), D),
        "w_up": nrm(ks[19], (DEPTH, N_EXPERTS, D, EXPERT_FF), D),
        "w_down": nrm(ks[20], (DEPTH, N_EXPERTS, EXPERT_FF, D), EXPERT_FF),
        "norm_f": gain(ks[21], (D,)),
    }


def reference(x, c, ctx, c_ctx, w_ada, b_ada, norm1, norm2, w_in, conv_qkv, a_log, dt_bias,
              gn_w, w_dn_out, conv_sc, w_sc_out, w_o, w_router, w_gate, w_up, w_down, norm_f):
    for l in range(DEPTH):
        ctx_out = l < DEPTH - 1
        mod = jax.nn.silu(c) @ w_ada[l] + b_ada[l]
        mod_c = jax.nn.silu(c_ctx) @ w_ada[l] + b_ada[l]
        sh1, sc1, gt1, sh2, sc2, gt2 = jnp.split(mod[:, None, :], 6, axis=-1)
        csh1, csc1, cgt1, csh2, csc2, cgt2 = jnp.split(mod_c, 6, axis=-1)

        h = modulate(rmsnorm(x, norm1[l]), sh1, sc1)
        hc = modulate(rmsnorm(ctx, norm1[l]), csh1, csc1)
        p = h @ w_in[l]
        pc = hc @ (w_in[l] if ctx_out else w_in[l, :, :N_STATE_COLS])
        oc, ol = bidirectional_deltanet(
            deltanet_inputs(pc[..., :N_STATE_COLS], conv_qkv[l], a_log[l], dt_bias[l]),
            deltanet_inputs(p[..., :N_STATE_COLS], conv_qkv[l], a_log[l], dt_bias[l]))
        x = x + gt1 * merge_branches(p, ol, gn_w[l], w_dn_out[l], conv_sc[l], w_sc_out[l], w_o[l], True)
        if ctx_out:
            ctx = ctx + cgt1 * merge_branches(pc, oc, gn_w[l], w_dn_out[l], conv_sc[l], w_sc_out[l],
                                              w_o[l], False)

        h = modulate(rmsnorm(x, norm2[l]), sh2, sc2)
        x = x + gt2 * expert_choice_ffn(h, w_router[l], w_gate[l], w_up[l], w_down[l])
        if ctx_out:
            hc = modulate(rmsnorm(ctx, norm2[l]), csh2, csc2)
            ctx = ctx + cgt2 * expert_choice_ffn(hc, w_router[l], w_gate[l], w_up[l], w_down[l])
    return rmsnorm(x, norm_f)
```

```python
import functools

import jax
import jax.numpy as jnp
from jax import lax
from jax.experimental import pallas as pl
from jax.experimental.pallas import tpu as pltpu

F32 = jnp.float32
BF16 = jnp.bfloat16

D_MODEL = 2048
CTX_LEN = 256
GRID_W = 64
DN_HEADS = 8
DN_HEAD_DIM = 128
DN_WIDTH = DN_HEADS * DN_HEAD_DIM
DN_CHUNK = 64
SC_WIDTH = 1024
N_EXPERTS = 16
EXPERT_FF = 2048
CAPACITY_FACTOR = 2
NORM_EPS = 1e-6
N_QKV = 3 * DN_WIDTH
N_BA = 4 * DN_HEADS
N_STATE_COLS = N_QKV + N_BA
LANES = 128
COL_Z = N_QKV
COL_SC = COL_Z + DN_WIDTH
COL_GA = COL_SC + 3 * SC_WIDTH
COL_GB = COL_GA + D_MODEL
N_MAIN = COL_GB + D_MODEL

VMEM_LIMIT = 56 * 1024 * 1024


def _cparams(sem):
    return pltpu.CompilerParams(dimension_semantics=sem, vmem_limit_bytes=VMEM_LIMIT)


def _dot(a, b):
    return jnp.dot(a, b, preferred_element_type=F32)


def _sigmoid(x):
    return 1.0 / (1.0 + jnp.exp(-x))


def _ada_kernel(c_ref, w_ref, b_ref, o_ref):
    c = c_ref[...]
    s = (c * _sigmoid(c)).astype(BF16)
    o_ref[...] = _dot(s, w_ref[...].astype(BF16)) + b_ref[...]


def _ada(c_all, w, b):
    rows, d = c_all.shape
    n = w.shape[1]
    tn = 1024
    return pl.pallas_call(
        _ada_kernel,
        grid=(n // tn,),
        in_specs=[pl.BlockSpec((rows, d), lambda j: (0, 0)),
                  pl.BlockSpec((d, tn), lambda j: (0, j)),
                  pl.BlockSpec((1, tn), lambda j: (0, j))],
        out_specs=pl.BlockSpec((rows, tn), lambda j: (0, j)),
        out_shape=jax.ShapeDtypeStruct((rows, n), F32),
        compiler_params=_cparams(("arbitrary",)),
        name="ada_ln",
    )(c_all, w, b.reshape(1, n))


def _in_proj_kernel(x_ref, sh_ref, sc_ref, nw_ref, w_ref, wba_ref, o_ref, oba_ref, h_scr):
    @pl.when(pl.program_id(1) == 0)
    def _():
        x = x_ref[...]
        y = x * lax.rsqrt(jnp.mean(x * x, axis=-1, keepdims=True) + NORM_EPS) * nw_ref[...]
        hb = (y * (1.0 + sc_ref[0]) + sh_ref[0]).astype(BF16)
        h_scr[...] = hb
        oba_ref[...] = _dot(hb, wba_ref[...])

    o_ref[...] = _dot(h_scr[...], w_ref[...]).astype(o_ref.dtype)


def _in_proj(x2d, shift, scale, norm_w, w_main, w_ba, n_out, rows_per_mod, tm, tn):
    m, d = x2d.shape
    nb = shift.shape[0]
    mod_idx = lambda i, j: ((i * tm) // rows_per_mod, 0, 0)
    return pl.pallas_call(
        _in_proj_kernel,
        grid=(m // tm, n_out // tn),
        in_specs=[pl.BlockSpec((tm, d), lambda i, j: (i, 0)),
                  pl.BlockSpec((1, 1, d), mod_idx),
                  pl.BlockSpec((1, 1, d), mod_idx),
                  pl.BlockSpec((1, d), lambda i, j: (0, 0)),
                  pl.BlockSpec((d, tn), lambda i, j: (0, j)),
                  pl.BlockSpec((d, LANES), lambda i, j: (0, 0))],
        out_specs=[pl.BlockSpec((tm, tn), lambda i, j: (i, j)),
                   pl.BlockSpec((tm, LANES), lambda i, j: (i, 0))],
        out_shape=[jax.ShapeDtypeStruct((m, n_out), BF16),
                   jax.ShapeDtypeStruct((m, LANES), F32)],
        scratch_shapes=[pltpu.VMEM((tm, d), BF16)],
        compiler_params=_cparams(("parallel", "arbitrary")),
        name="in_proj",
    )(x2d, shift.reshape(nb, 1, d), scale.reshape(nb, 1, d), norm_w.reshape(1, d), w_main, w_ba)


def _merge_kernel(a_ref, b_ref, ga_ref, gb_ref, wdn_ref, wsc_ref, wo_ref, x_ref, gt_ref, sh_ref, sc_ref,
                  nw_ref, wr_ref, x1_ref, h2_ref, lg_ref, acc):
    j = pl.program_id(1)
    ya = _dot(a_ref[...], wdn_ref[...])
    yb = _dot(b_ref[...], wsc_ref[...])
    mix = _sigmoid(ga_ref[...].astype(F32)) * ya + _sigmoid(gb_ref[...].astype(F32)) * yb
    contrib = _dot(mix.astype(BF16), wo_ref[...])

    @pl.when(j == 0)
    def _():
        acc[...] = contrib

    @pl.when(j > 0)
    def _():
        acc[...] += contrib

    @pl.when(j == pl.num_programs(1) - 1)
    def _():
        x1 = x_ref[...] + gt_ref[0] * acc[...]
        x1_ref[...] = x1
        y = x1 * lax.rsqrt(jnp.mean(x1 * x1, axis=-1, keepdims=True) + NORM_EPS) * nw_ref[...]
        hb = (y * (1.0 + sc_ref[0]) + sh_ref[0]).astype(BF16)
        h2_ref[...] = hb
        lg_ref[...] = _dot(hb, wr_ref[...])


def _merge(a_in, b_in, p_main, w_dn, w_sc, w_o, x2d, gate, shift, scale, norm_w, w_r, rows_per_mod, tm, tj):
    m, d = x2d.shape
    nb = gate.shape[0]
    ga0, gb0 = COL_GA // tj, COL_GB // tj
    mod_idx = lambda i, j: ((i * tm) // rows_per_mod, 0, 0)
    return pl.pallas_call(
        _merge_kernel,
        grid=(m // tm, d // tj),
        in_specs=[pl.BlockSpec((tm, DN_WIDTH), lambda i, j: (i, 0)),
                  pl.BlockSpec((tm, SC_WIDTH), lambda i, j: (i, 0)),
                  pl.BlockSpec((tm, tj), lambda i, j: (i, ga0 + j)),
                  pl.BlockSpec((tm, tj), lambda i, j: (i, gb0 + j)),
                  pl.BlockSpec((DN_WIDTH, tj), lambda i, j: (0, j)),
                  pl.BlockSpec((SC_WIDTH, tj), lambda i, j: (0, j)),
                  pl.BlockSpec((tj, d), lambda i, j: (j, 0)),
                  pl.BlockSpec((tm, d), lambda i, j: (i, 0)),
                  pl.BlockSpec((1, 1, d), mod_idx),
                  pl.BlockSpec((1, 1, d), mod_idx),
                  pl.BlockSpec((1, 1, d), mod_idx),
                  pl.BlockSpec((1, d), lambda i, j: (0, 0)),
                  pl.BlockSpec((d, LANES), lambda i, j: (0, 0))],
        out_specs=[pl.BlockSpec((tm, d), lambda i, j: (i, 0)),
                   pl.BlockSpec((tm, d), lambda i, j: (i, 0)),
                   pl.BlockSpec((tm, LANES), lambda i, j: (i, 0))],
        out_shape=[jax.ShapeDtypeStruct((m, d), F32),
                   jax.ShapeDtypeStruct((m, d), BF16),
                   jax.ShapeDtypeStruct((m, LANES), F32)],
        scratch_shapes=[pltpu.VMEM((tm, d), F32)],
        compiler_params=_cparams(("parallel", "arbitrary")),
        name="merge_out",
    )(a_in, b_in, p_main, p_main, w_dn, w_sc, w_o, x2d, gate.reshape(nb, 1, d), shift.reshape(nb, 1, d),
      scale.reshape(nb, 1, d), norm_w.reshape(1, d), w_r)


def _moe_kernel(xs_ref, wg_ref, wu_ref, wd_ref, aff_ref, o_ref, acc):
    f = pl.program_id(2)
    x = xs_ref[0]
    g = _dot(x, wg_ref[0])
    u = _dot(x, wu_ref[0])
    hid = (g * _sigmoid(g) * u).astype(BF16)
    contrib = _dot(hid, wd_ref[0])

    @pl.when(f == 0)
    def _():
        acc[...] = contrib

    @pl.when(f > 0)
    def _():
        acc[...] += contrib

    @pl.when(f == pl.num_programs(2) - 1)
    def _():
        o_ref[0] = acc[...] * aff_ref[0]


def _moe_ffn(xs, w_gate, w_up, w_down, aff, tm, tf):
    e, m, d = xs.shape
    ff = w_gate.shape[2]
    return pl.pallas_call(
        _moe_kernel,
        grid=(e, m // tm, ff // tf),
        in_specs=[pl.BlockSpec((1, tm, d), lambda e_, i, f: (e_, i, 0)),
                  pl.BlockSpec((1, d, tf), lambda e_, i, f: (e_, 0, f)),
                  pl.BlockSpec((1, d, tf), lambda e_, i, f: (e_, 0, f)),
                  pl.BlockSpec((1, tf, d), lambda e_, i, f: (e_, f, 0)),
                  pl.BlockSpec((1, tm, 1), lambda e_, i, f: (e_, i, 0))],
        out_specs=pl.BlockSpec((1, tm, d), lambda e_, i, f: (e_, i, 0)),
        out_shape=jax.ShapeDtypeStruct((e, m, d), F32),
        scratch_shapes=[pltpu.VMEM((tm, d), F32)],
        compiler_params=_cparams(("parallel", "parallel", "arbitrary")),
        name="moe_ffn",
    )(xs, w_gate, w_up, w_down, aff)


def _final_kernel(x_ref, y_ref, gt_ref, nw_ref, o_ref):
    x2 = x_ref[...] + gt_ref[0] * y_ref[...]
    o_ref[...] = x2 * lax.rsqrt(jnp.mean(x2 * x2, axis=-1, keepdims=True) + NORM_EPS) * nw_ref[...]


def _final(x1, y, gate, norm_w, rows_per_mod, tm):
    m, d = x1.shape
    nb = gate.shape[0]
    return pl.pallas_call(
        _final_kernel,
        grid=(m // tm,),
        in_specs=[pl.BlockSpec((tm, d), lambda i: (i, 0)),
                  pl.BlockSpec((tm, d), lambda i: (i, 0)),
                  pl.BlockSpec((1, 1, d), lambda i: ((i * tm) // rows_per_mod, 0, 0)),
                  pl.BlockSpec((1, d), lambda i: (0, 0))],
        out_specs=pl.BlockSpec((tm, d), lambda i: (i, 0)),
        out_shape=jax.ShapeDtypeStruct((m, d), F32),
        compiler_params=_cparams(("parallel",)),
        name="final_norm",
    )(x1, y, gate.reshape(nb, 1, d), norm_w.reshape(1, d))


def _dwconv_centred(x, w):
    k = w.shape[0]
    p = k // 2
    return lax.conv_general_dilated(
        x, w[:, None, :].astype(x.dtype), window_strides=(1,), padding=[(p, p)],
        dimension_numbers=("NWC", "WIO", "NWC"), feature_group_count=x.shape[-1])


def _l2norm(x):
    return x * lax.rsqrt(jnp.sum(x * x, axis=-1, keepdims=True) + NORM_EPS)


def _gated_delta_rule(q, k, v, beta, g, s0):
    b, h, t, dk = q.shape
    dv = v.shape[-1]
    L = DN_CHUNK
    n = t // L
    q, k, v = (a.reshape(b, h, n, L, a.shape[-1]) for a in (q, k, v))
    beta = beta.reshape(b, h, n, L)
    gc = jnp.cumsum(g.reshape(b, h, n, L), axis=-1)
    pos = jnp.arange(L)
    incl = pos[:, None] >= pos[None, :]
    strict = pos[:, None] > pos[None, :]
    decay = jnp.exp(jnp.where(incl, gc[..., :, None] - gc[..., None, :], -jnp.inf))
    kb = k * beta[..., None]
    a_low = jnp.where(strict, jnp.einsum("bhnid,bhnjd->bhnij", kb, k) * decay, 0.0)
    eye = jnp.eye(L, dtype=q.dtype)
    rhs = jnp.concatenate([v * beta[..., None], kb * jnp.exp(gc)[..., None]], axis=-1)
    sol = lax.linalg.triangular_solve(a_low + eye, rhs, left_side=True, lower=True, unit_diagonal=True)
    u, w = sol[..., :dv], sol[..., dv:]
    qk = jnp.einsum("bhnid,bhnjd->bhnij", q, k) * decay
    q_dec = q * jnp.exp(gc)[..., None]
    k_dec = k * jnp.exp(gc[..., -1:] - gc)[..., None]
    g_last = jnp.exp(gc[..., -1])
    xs = tuple(jnp.moveaxis(a, 2, 0) for a in (q_dec, k_dec, u, w, qk, g_last))

    def step(s, inp):
        qd, kd, ui, wi, qki, gl = inp
        v_new = ui - jnp.einsum("bhld,bhdv->bhlv", wi, s)
        o = jnp.einsum("bhld,bhdv->bhlv", qd, s) + jnp.einsum("bhij,bhjv->bhiv", qki, v_new)
        s = s * gl[..., None, None] + jnp.einsum("bhld,bhlv->bhdv", kd, v_new)
        return s, o

    s_final, o = lax.scan(step, s0, xs)
    o = jnp.moveaxis(o, 0, 2).reshape(b, h, t, dv)
    return o, s_final


def _deltanet_inputs(qkv, ba, conv_w, a_log, dt_bias):
    b, t, _ = qkv.shape
    qkv = jax.nn.silu(_dwconv_centred(qkv.astype(F32), conv_w))
    heads = lambda a: a.reshape(b, t, DN_HEADS, DN_HEAD_DIM).transpose(0, 2, 1, 3)
    q, k, v = (heads(a) for a in jnp.split(qkv, 3, axis=-1))
    q = _l2norm(q) * (DN_HEAD_DIM ** -0.5)
    k = _l2norm(k)
    dirs = lambda a: a.reshape(b, t, 2, DN_HEADS).transpose(2, 0, 3, 1)
    beta = jax.nn.sigmoid(dirs(ba[..., :2 * DN_HEADS]))
    a_raw = dirs(ba[..., 2 * DN_HEADS:N_BA])
    g = -jnp.exp(a_log)[:, None, :, None] * jax.nn.softplus(a_raw + dt_bias[:, None, :, None])
    return q, k, v, beta, g


def _bidirectional_deltanet(ctx_in, lat_in):
    qc, kc, vc, bc, gc = ctx_in
    ql, kl, vl, bl, gl = lat_in
    flip = lambda a: jnp.flip(a, axis=2)
    b, h, _, dk = qc.shape
    s0 = jnp.zeros((b, h, dk, vc.shape[-1]), F32)
    _, sc_f = _gated_delta_rule(qc, kc, vc, bc[0], gc[0], s0)
    ol_f, _ = _gated_delta_rule(ql, kl, vl, bl[0], gl[0], sc_f)
    _, sc_b = _gated_delta_rule(flip(qc), flip(kc), flip(vc), flip(bc[1]), flip(gc[1]), s0)
    ol_b, _ = _gated_delta_rule(flip(ql), flip(kl), flip(vl), flip(bl[1]), flip(gl[1]), sc_b)
    return ol_f + flip(ol_b)


def kernel(x, c, ctx, c_ctx, w_ada, b_ada, norm1, norm2, w_in, conv_qkv, a_log, dt_bias, gn_w, w_dn_out,
           conv_sc, w_sc_out, w_o, w_router, w_gate, w_up, w_down, norm_f):
    bsz, seq, d = x.shape
    m = bsz * seq
    l = 0

    pad = (-(bsz + 1)) % 8
    c_all = jnp.concatenate([c, c_ctx[None, :], jnp.zeros((pad, d), F32)], axis=0)
    mod_all = _ada(c_all, w_ada[l], b_ada[l])
    sh1, sc1, gt1, sh2, sc2, gt2 = jnp.split(mod_all[:bsz], 6, axis=-1)
    csh1, csc1 = mod_all[bsz:bsz + 1, :d], mod_all[bsz:bsz + 1, d:2 * d]

    wi = w_in[l]
    w_main = jnp.concatenate([wi[:, :N_QKV], wi[:, N_STATE_COLS:]], axis=1).astype(BF16)
    w_ba = jnp.pad(wi[:, N_QKV:N_STATE_COLS], ((0, 0), (0, LANES - N_BA))).astype(BF16)

    x2d = x.reshape(m, d)
    p_main, p_ba = _in_proj(x2d, sh1, sc1, norm1[l], w_main, w_ba, N_MAIN, seq, 1024, 512)
    pc_qkv, pc_ba = _in_proj(ctx.reshape(bsz * CTX_LEN, d), csh1, csc1, norm1[l], w_main, w_ba, N_QKV,
                             bsz * CTX_LEN, 1024, 512)

    lat_in = _deltanet_inputs(p_main[:, :N_QKV].reshape(bsz, seq, N_QKV), p_ba[:, :N_BA].reshape(bsz, seq, N_BA),
                              conv_qkv[l], a_log[l], dt_bias[l])
    ctx_in = _deltanet_inputs(pc_qkv.reshape(bsz, CTX_LEN, N_QKV), pc_ba[:, :N_BA].reshape(bsz, CTX_LEN, N_BA),
                              conv_qkv[l], a_log[l], dt_bias[l])
    o = _bidirectional_deltanet(ctx_in, lat_in)
    o = o.transpose(0, 2, 1, 3)
    o = o * lax.rsqrt(jnp.mean(o * o, axis=-1, keepdims=True) + NORM_EPS) * gn_w[l]
    z = p_main[:, COL_Z:COL_SC].astype(F32)
    a_in = (o.reshape(m, DN_WIDTH) * jax.nn.silu(z)).astype(BF16)

    sc = p_main[:, COL_SC:COL_GA].astype(F32)
    bg, cg, xin = jnp.split(sc, 3, axis=-1)
    zc = (cg * xin).reshape(m // GRID_W, GRID_W, SC_WIDTH)
    zc = _dwconv_centred(zc, conv_sc[l]).reshape(m, SC_WIDTH)
    b_in = (bg * zc).astype(BF16)

    w_r = jnp.pad(w_router[l], ((0, 0), (0, LANES - N_EXPERTS))).astype(BF16)
    x1, h2, logits = _merge(a_in, b_in, p_main, w_dn_out[l].astype(BF16), w_sc_out[l].astype(BF16),
                            w_o[l].astype(BF16), x2d, gt1, sh2, sc2, norm2[l], w_r, seq, 512, 512)

    cap = CAPACITY_FACTOR * seq // N_EXPERTS
    aff = jax.nn.softmax(logits[:, :N_EXPERTS].reshape(bsz, seq, N_EXPERTS), axis=-1)
    top_aff, idx = lax.top_k(aff.transpose(0, 2, 1), cap)
    flat_idx = (idx + (jnp.arange(bsz, dtype=idx.dtype) * seq)[:, None, None]).transpose(1, 0, 2)
    flat_idx = flat_idx.reshape(N_EXPERTS, bsz * cap)
    xs = jnp.take(h2, flat_idx, axis=0)
    aff_e = top_aff.transpose(1, 0, 2).reshape(N_EXPERTS, bsz * cap, 1)
    ye = _moe_ffn(xs, w_gate[l].astype(BF16), w_up[l].astype(BF16), w_down[l].astype(BF16), aff_e, 1024, 512)
    y = jnp.zeros((m, d), F32).at[flat_idx.reshape(-1)].add(ye.reshape(-1, d))

    out = _final(x1, y, gt2, norm_f, seq, 512)
    return out.reshape(bsz, seq, d)
```

```python
import jax
import jax.numpy as jnp
from jax import lax
from jax.experimental import pallas as pl
from jax.experimental.pallas import tpu as pltpu

F32 = jnp.float32
BF16 = jnp.bfloat16

D_MODEL = 2048
CTX_LEN = 256
GRID_W = 64
DN_HEADS = 8
DN_HEAD_DIM = 128
DN_WIDTH = DN_HEADS * DN_HEAD_DIM
SC_WIDTH = 1024
N_EXPERTS = 16
EXPERT_FF = 2048
CAPACITY_FACTOR = 2
NORM_EPS = 1e-6
N_QKV = 3 * DN_WIDTH
N_BA = 4 * DN_HEADS
N_STATE_COLS = N_QKV + N_BA
LANES = 128
SUBLANES = 8
COL_Z = N_QKV
COL_SC = COL_Z + DN_WIDTH
COL_GA = COL_SC + 3 * SC_WIDTH
COL_GB = COL_GA + D_MODEL
N_MAIN = COL_GB + D_MODEL
DN_BLOCK = LANES
DN_INVERSE_LEVELS = 7

VMEM_LIMIT = 56 * 1024 * 1024


def _cparams(sem):
    return pltpu.CompilerParams(dimension_semantics=sem, vmem_limit_bytes=VMEM_LIMIT)


def _dot(a, b):
    return jnp.dot(a, b, preferred_element_type=F32)


def _dot_nt(a, b):
    return lax.dot_general(a, b, (((1,), (1,)), ((), ())), preferred_element_type=F32)


def _sigmoid(x):
    return 1.0 / (1.0 + jnp.exp(-x))


def _softplus(x):
    return jnp.maximum(x, 0.0) + jnp.log(1.0 + jnp.exp(-jnp.abs(x)))


def _ada_kernel(c_ref, w_ref, b_ref, o_ref):
    c = c_ref[...]
    s = (c * _sigmoid(c)).astype(BF16)
    o_ref[...] = _dot(s, w_ref[...].astype(BF16)) + b_ref[...]


def _ada(c_all, w, b):
    rows, d = c_all.shape
    n = w.shape[1]
    tn = 1024
    return pl.pallas_call(
        _ada_kernel,
        grid=(n // tn,),
        in_specs=[pl.BlockSpec((rows, d), lambda j: (0, 0)),
                  pl.BlockSpec((d, tn), lambda j: (0, j)),
                  pl.BlockSpec((1, tn), lambda j: (0, j))],
        out_specs=pl.BlockSpec((rows, tn), lambda j: (0, j)),
        out_shape=jax.ShapeDtypeStruct((rows, n), F32),
        compiler_params=_cparams(("arbitrary",)),
        name="ada_ln",
    )(c_all, w, b.reshape(1, n))


def _gate_act(raw, idx, alog, dtb):
    return jnp.where(idx < 2 * DN_HEADS, _sigmoid(raw), -jnp.exp(alog) * _softplus(raw + dtb))


def _in_proj_kernel(x_ref, sh_ref, sc_ref, nw_ref, w_ref, wba_ref, wbat_ref, alog_ref, dtb_ref, alogt_ref,
                    dtbt_ref, o_ref, gb_ref, gbt_ref, h_scr):
    @pl.when(pl.program_id(1) == 0)
    def _():
        x = x_ref[...]
        y = x * lax.rsqrt(jnp.mean(x * x, axis=-1, keepdims=True) + NORM_EPS) * nw_ref[...]
        hb = (y * (1.0 + sc_ref[0]) + sh_ref[0]).astype(BF16)
        h_scr[...] = hb
        raw = _dot(hb, wba_ref[...])
        gb_ref[...] = _gate_act(raw, lax.broadcasted_iota(jnp.int32, raw.shape, 1), alog_ref[...], dtb_ref[...])
        raw_t = _dot_nt(wbat_ref[...], hb)
        gbt_ref[...] = _gate_act(raw_t, lax.broadcasted_iota(jnp.int32, raw_t.shape, 0), alogt_ref[...],
                                 dtbt_ref[...])

    o_ref[...] = _dot(h_scr[...], w_ref[...]).astype(o_ref.dtype)


def _in_proj(x2d, shift, scale, norm_w, w_main, w_ba, w_ba_t, alog, dtb, n_out, rows_per_mod, tm, tn):
    m, d = x2d.shape
    nb = shift.shape[0]
    mod_idx = lambda i, j: ((i * tm) // rows_per_mod, 0, 0)
    const = lambda i, j: (0, 0)
    return pl.pallas_call(
        _in_proj_kernel,
        grid=(m // tm, n_out // tn),
        in_specs=[pl.BlockSpec((tm, d), lambda i, j: (i, 0)),
                  pl.BlockSpec((1, 1, d), mod_idx),
                  pl.BlockSpec((1, 1, d), mod_idx),
                  pl.BlockSpec((1, d), const),
                  pl.BlockSpec((d, tn), lambda i, j: (0, j)),
                  pl.BlockSpec((d, LANES), const),
                  pl.BlockSpec((LANES, d), const),
                  pl.BlockSpec((1, LANES), const),
                  pl.BlockSpec((1, LANES), const),
                  pl.BlockSpec((LANES, 1), const),
                  pl.BlockSpec((LANES, 1), const)],
        out_specs=[pl.BlockSpec((tm, tn), lambda i, j: (i, j)),
                   pl.BlockSpec((tm, LANES), lambda i, j: (i, 0)),
                   pl.BlockSpec((LANES, tm), lambda i, j: (0, i))],
        out_shape=[jax.ShapeDtypeStruct((m, n_out), BF16),
                   jax.ShapeDtypeStruct((m, LANES), F32),
                   jax.ShapeDtypeStruct((LANES, m), F32)],
        scratch_shapes=[pltpu.VMEM((tm, d), BF16)],
        compiler_params=_cparams(("parallel", "arbitrary")),
        name="in_proj",
    )(x2d, shift.reshape(nb, 1, d), scale.reshape(nb, 1, d), norm_w.reshape(1, d), w_main, w_ba, w_ba_t,
      alog.reshape(1, LANES), dtb.reshape(1, LANES), alog.reshape(LANES, 1), dtb.reshape(LANES, 1))


def _conv_silu(x, w, t):
    row = lax.broadcasted_iota(jnp.int32, x.shape, 0)
    prev = jnp.where(row == 0, 0.0, pltpu.roll(x, 1, 0))
    nxt = jnp.where(row == t - 1, 0.0, pltpu.roll(x, t - 1, 0))
    y = w[0:1, :] * prev + w[1:2, :] * x + w[2:3, :] * nxt
    return y * _sigmoid(y)


def _chunk_cumsum(x, axis, reverse):
    n = x.shape[axis]
    pos = lax.broadcasted_iota(jnp.int32, x.shape, axis) % DN_BLOCK
    s = 1
    while s < DN_BLOCK:
        if reverse:
            x = x + jnp.where(pos + s < DN_BLOCK, pltpu.roll(x, n - s, axis), 0.0)
        else:
            x = x + jnp.where(pos >= s, pltpu.roll(x, s, axis), 0.0)
        s *= 2
    return x


def _dn_load(t, h, q_ref, k_ref, v_ref, gb_ref, gbt_ref, cwq_ref, cwk_ref, cwv_ref, q_s, k_s, v_s, bf_s, bb_s, gf_s,
             gr_s, gfrow_s, grrow_s):
    q = _conv_silu(q_ref[...].astype(F32), cwq_ref[...], t)
    q_s[0:t, :] = q * lax.rsqrt(jnp.sum(q * q, axis=-1, keepdims=True) + NORM_EPS) * (DN_HEAD_DIM ** -0.5)
    k = _conv_silu(k_ref[...].astype(F32), cwk_ref[...], t)
    k_s[0:t, :] = k * lax.rsqrt(jnp.sum(k * k, axis=-1, keepdims=True) + NORM_EPS)
    v_s[0:t, :] = _conv_silu(v_ref[...].astype(F32), cwv_ref[...], t)
    gb = gb_ref[...]
    lane = lax.broadcasted_iota(jnp.int32, gb.shape, 1)
    for dst, col, reverse in ((bf_s, h, None), (bb_s, DN_HEADS + h, None), (gf_s, 2 * DN_HEADS + h, False),
                              (gr_s, 3 * DN_HEADS + h, True)):
        picked = jnp.sum(jnp.where(lane == col, gb, 0.0), axis=-1, keepdims=True)
        val = jnp.broadcast_to(picked, gb.shape)
        dst[0:t, :] = val if reverse is None else _chunk_cumsum(val, 0, reverse)
    for dst, base, reverse in ((gfrow_s, 2 * DN_HEADS, False), (grrow_s, 3 * DN_HEADS, True)):
        grp = gbt_ref[base:base + DN_HEADS, :]
        sub = lax.broadcasted_iota(jnp.int32, grp.shape, 0)
        picked = jnp.sum(jnp.where(sub == h, grp, 0.0), axis=0, keepdims=True)
        dst[:, 0:t] = _chunk_cumsum(jnp.broadcast_to(picked, grp.shape), 1, reverse)


def _dn_chunk_prep(c, reverse, need_o, q_s, k_s, v_s, b_s, g_s, grow_s, u_s, w_s, qk_s, qd_s, kdt_s, egl_s):
    L = DN_BLOCK
    r0 = pl.multiple_of(c * L, L)
    rows = pl.ds(r0, L)
    q, k, v = q_s[rows, :], k_s[rows, :], v_s[rows, :]
    beta = b_s[rows, :]
    gc_col = g_s[rows, :]
    gc_row = jnp.broadcast_to(grow_s[0:1, pl.ds(r0, L)], (L, L))
    ri = lax.broadcasted_iota(jnp.int32, (L, L), 0)
    ci = lax.broadcasted_iota(jnp.int32, (L, L), 1)
    incl, strict = (ri <= ci, ri < ci) if reverse else (ri >= ci, ri > ci)
    decay = jnp.exp(jnp.where(incl, gc_col - gc_row, -jnp.inf))
    kb = k * beta
    kb16, k16 = kb.astype(BF16), k.astype(BF16)
    a_tri = jnp.where(strict, _dot_nt(kb16, k16) * decay, 0.0)
    t_inv = jnp.where(ri == ci, 1.0, 0.0)
    blk = ri ^ ci
    for level in range(DN_INVERSE_LEVELS):
        a_off = jnp.where((blk >> level) == 1, a_tri, 0.0).astype(BF16)
        t16 = t_inv.astype(BF16)
        t_inv = t_inv - _dot(t16, _dot(a_off, t16).astype(BF16))
    egc = jnp.exp(gc_col)
    rhs = jnp.concatenate([v * beta, kb * egc], axis=-1).astype(BF16)
    sol = _dot(t_inv.astype(BF16), rhs)
    u_s[rows, :] = sol[:, :DN_HEAD_DIM]
    w_s[rows, :] = sol[:, DN_HEAD_DIM:].astype(BF16)
    g_last = gc_col[0:1, :] if reverse else gc_col[L - 1:L, :]
    kd = k * jnp.exp(g_last - gc_col)
    kdt_s[rows, :] = jnp.transpose(kd).astype(BF16)
    egl_s[pl.ds(pl.multiple_of(c * SUBLANES, SUBLANES), SUBLANES), :] = jnp.broadcast_to(jnp.exp(g_last),
                                                                                      (SUBLANES, LANES))
    if need_o:
        qk_s[rows, :] = (_dot_nt(q.astype(BF16), k16) * decay).astype(BF16)
        qd_s[rows, :] = (q * egc).astype(BF16)


def _dn_chunk_step(c, s, need_o, u_s, w_s, qk_s, qd_s, kdt_s, egl_s, o_s):
    L = DN_BLOCK
    rows = pl.ds(pl.multiple_of(c * L, L), L)
    s16 = s.astype(BF16)
    v_new = (u_s[rows, :] - _dot(w_s[rows, :], s16)).astype(BF16)
    if need_o:
        o_s[rows, :] += _dot(qd_s[rows, :], s16) + _dot(qk_s[rows, :], v_new)
    egl = egl_s[pl.ds(pl.multiple_of(c * SUBLANES, SUBLANES), 1), :]
    return s * egl + _dot(kdt_s[rows, :], v_new)


def _dn_scan(n, need_o, state, q_s, k_s, v_s, bf_s, bb_s, gf_s, gr_s, gfrow_s, grrow_s, fwd_s, bwd_s, o_s):
    def prep(c, carry):
        _dn_chunk_prep(c, False, need_o, q_s, k_s, v_s, bf_s, gf_s, gfrow_s, *fwd_s)
        _dn_chunk_prep(c, True, need_o, q_s, k_s, v_s, bb_s, gr_s, grrow_s, *bwd_s)
        return carry

    lax.fori_loop(0, n, prep, 0)

    def step(c, carry):
        s_f, s_b = carry
        s_f = _dn_chunk_step(c, s_f, need_o, *fwd_s, o_s)
        s_b = _dn_chunk_step(n - 1 - c, s_b, need_o, *bwd_s, o_s)
        return s_f, s_b

    return lax.fori_loop(0, n, step, state)


def _deltanet_kernel(qc_ref, kc_ref, vc_ref, gbc_ref, gbtc_ref, ql_ref, kl_ref, vl_ref, zl_ref, gbl_ref, gbtl_ref,
                     cwq_ref, cwk_ref, cwv_ref, gnw_ref, out_ref,
                     q_s, k_s, v_s, bf_s, bb_s, gf_s, gr_s, gfrow_s, grrow_s,
                     uf_s, wf_s, qkf_s, qdf_s, kdtf_s, eglf_s, ub_s, wb_s, qkb_s, qdb_s, kdtb_s, eglb_s, o_s):
    h = pl.program_id(1)
    t_ctx, t_lat = qc_ref.shape[0], ql_ref.shape[0]
    vec_s = (q_s, k_s, v_s, bf_s, bb_s, gf_s, gr_s, gfrow_s, grrow_s)
    fwd_s = (uf_s, wf_s, qkf_s, qdf_s, kdtf_s, eglf_s)
    bwd_s = (ub_s, wb_s, qkb_s, qdb_s, kdtb_s, eglb_s)
    zero = jnp.zeros((DN_HEAD_DIM, DN_HEAD_DIM), F32)

    _dn_load(t_ctx, h, qc_ref, kc_ref, vc_ref, gbc_ref, gbtc_ref, cwq_ref, cwk_ref, cwv_ref, *vec_s)
    state = _dn_scan(t_ctx // DN_BLOCK, False, (zero, zero), *vec_s, fwd_s, bwd_s, o_s)

    _dn_load(t_lat, h, ql_ref, kl_ref, vl_ref, gbl_ref, gbtl_ref, cwq_ref, cwk_ref, cwv_ref, *vec_s)
    o_s[...] = jnp.zeros_like(o_s)
    _dn_scan(t_lat // DN_BLOCK, True, state, *vec_s, fwd_s, bwd_s, o_s)

    o = o_s[...]
    z = zl_ref[...].astype(F32)
    o = o * lax.rsqrt(jnp.mean(o * o, axis=-1, keepdims=True) + NORM_EPS) * gnw_ref[...]
    out_ref[...] = (o * (z * _sigmoid(z))).astype(out_ref.dtype)


def _deltanet(pc_qkv, gb_c, gbt_c, p_main, gb_l, gbt_l, conv_w, gn_w, bsz):
    t_ctx = pc_qkv.shape[0] // bsz
    t_lat = p_main.shape[0] // bsz
    hd, nh = DN_HEAD_DIM, DN_HEADS
    col = lambda off: (lambda b, h: (b, off + h))
    row0 = lambda b, h: (b, 0)
    col0 = lambda b, h: (0, b)
    wcol = lambda off: (lambda b, h: (0, off + h))
    vec = lambda dt: pltpu.VMEM((t_lat, hd), dt)
    per_dir = [vec(F32), vec(BF16), vec(BF16), vec(BF16), vec(BF16),
               pltpu.VMEM((t_lat // DN_BLOCK * SUBLANES, LANES), F32)]
    return pl.pallas_call(
        _deltanet_kernel,
        grid=(bsz, nh),
        in_specs=[pl.BlockSpec((t_ctx, hd), col(0)), pl.BlockSpec((t_ctx, hd), col(nh)),
                  pl.BlockSpec((t_ctx, hd), col(2 * nh)),
                  pl.BlockSpec((t_ctx, LANES), row0), pl.BlockSpec((LANES, t_ctx), col0),
                  pl.BlockSpec((t_lat, hd), col(0)), pl.BlockSpec((t_lat, hd), col(nh)),
                  pl.BlockSpec((t_lat, hd), col(2 * nh)), pl.BlockSpec((t_lat, hd), col(3 * nh)),
                  pl.BlockSpec((t_lat, LANES), row0), pl.BlockSpec((LANES, t_lat), col0),
                  pl.BlockSpec((3, hd), wcol(0)), pl.BlockSpec((3, hd), wcol(nh)),
                  pl.BlockSpec((3, hd), wcol(2 * nh)),
                  pl.BlockSpec((1, hd), lambda b, h: (0, 0))],
        out_specs=pl.BlockSpec((t_lat, hd), lambda b, h: (b, h)),
        out_shape=jax.ShapeDtypeStruct((p_main.shape[0], DN_WIDTH), BF16),
        scratch_shapes=[vec(F32)] * 7 + [pltpu.VMEM((SUBLANES, t_lat), F32)] * 2 + per_dir + per_dir + [vec(F32)],
        compiler_params=_cparams(("parallel", "arbitrary")),
        name="deltanet",
    )(pc_qkv, pc_qkv, pc_qkv, gb_c, gbt_c, p_main, p_main, p_main, p_main, gb_l, gbt_l,
      conv_w, conv_w, conv_w, gn_w.reshape(1, hd))


def _merge_kernel(a_ref, bg_ref, cg_ref, xin_ref, cw_ref, ga_ref, gb_ref, wdn_ref, wsc_ref, wo_ref, x_ref, gt_ref,
                  sh_ref, sc_ref, nw_ref, wr_ref, x1_ref, h2_ref, lg_ref, acc, b_scr):
    j = pl.program_id(1)

    @pl.when(j == 0)
    def _():
        zc = cg_ref[...].astype(F32) * xin_ref[...].astype(F32)
        tm = zc.shape[0]
        pos = lax.broadcasted_iota(jnp.int32, zc.shape, 0) % GRID_W
        prev = jnp.where(pos == 0, 0.0, pltpu.roll(zc, 1, 0))
        nxt = jnp.where(pos == GRID_W - 1, 0.0, pltpu.roll(zc, tm - 1, 0))
        cw = cw_ref[...]
        conv = cw[0:1, :] * prev + cw[1:2, :] * zc + cw[2:3, :] * nxt
        b_scr[...] = (bg_ref[...].astype(F32) * conv).astype(BF16)

    ya = _dot(a_ref[...], wdn_ref[...])
    yb = _dot(b_scr[...], wsc_ref[...])
    mix = _sigmoid(ga_ref[...].astype(F32)) * ya + _sigmoid(gb_ref[...].astype(F32)) * yb
    contrib = _dot(mix.astype(BF16), wo_ref[...])

    @pl.when(j == 0)
    def _():
        acc[...] = contrib

    @pl.when(j > 0)
    def _():
        acc[...] += contrib

    @pl.when(j == pl.num_programs(1) - 1)
    def _():
        x1 = x_ref[...] + gt_ref[0] * acc[...]
        x1_ref[...] = x1
        y = x1 * lax.rsqrt(jnp.mean(x1 * x1, axis=-1, keepdims=True) + NORM_EPS) * nw_ref[...]
        hb = (y * (1.0 + sc_ref[0]) + sh_ref[0]).astype(BF16)
        h2_ref[...] = hb
        lg_ref[...] = _dot(hb, wr_ref[...])


def _merge(a_in, p_main, conv_sc, w_dn, w_sc, w_o, x2d, gate, shift, scale, norm_w, w_r, rows_per_mod, tm, tj):
    m, d = x2d.shape
    nb = gate.shape[0]
    ga0, gb0 = COL_GA // tj, COL_GB // tj
    sc0 = COL_SC // SC_WIDTH
    mod_idx = lambda i, j: ((i * tm) // rows_per_mod, 0, 0)
    const = lambda i, j: (0, 0)
    rowblk = lambda i, j: (i, 0)
    return pl.pallas_call(
        _merge_kernel,
        grid=(m // tm, d // tj),
        in_specs=[pl.BlockSpec((tm, DN_WIDTH), rowblk),
                  pl.BlockSpec((tm, SC_WIDTH), lambda i, j: (i, sc0)),
                  pl.BlockSpec((tm, SC_WIDTH), lambda i, j: (i, sc0 + 1)),
                  pl.BlockSpec((tm, SC_WIDTH), lambda i, j: (i, sc0 + 2)),
                  pl.BlockSpec((3, SC_WIDTH), const),
                  pl.BlockSpec((tm, tj), lambda i, j: (i, ga0 + j)),
                  pl.BlockSpec((tm, tj), lambda i, j: (i, gb0 + j)),
                  pl.BlockSpec((DN_WIDTH, tj), lambda i, j: (0, j)),
                  pl.BlockSpec((SC_WIDTH, tj), lambda i, j: (0, j)),
                  pl.BlockSpec((tj, d), lambda i, j: (j, 0)),
                  pl.BlockSpec((tm, d), rowblk),
                  pl.BlockSpec((1, 1, d), mod_idx),
                  pl.BlockSpec((1, 1, d), mod_idx),
                  pl.BlockSpec((1, 1, d), mod_idx),
                  pl.BlockSpec((1, d), const),
                  pl.BlockSpec((d, LANES), const)],
        out_specs=[pl.BlockSpec((tm, d), rowblk),
                   pl.BlockSpec((tm, d), rowblk),
                   pl.BlockSpec((tm, LANES), rowblk)],
        out_shape=[jax.ShapeDtypeStruct((m, d), F32),
                   jax.ShapeDtypeStruct((m, d), BF16),
                   jax.ShapeDtypeStruct((m, LANES), F32)],
        scratch_shapes=[pltpu.VMEM((tm, d), F32), pltpu.VMEM((tm, SC_WIDTH), BF16)],
        compiler_params=_cparams(("parallel", "arbitrary")),
        name="merge_out",
    )(a_in, p_main, p_main, p_main, conv_sc, p_main, p_main, w_dn, w_sc, w_o, x2d, gate.reshape(nb, 1, d),
      shift.reshape(nb, 1, d), scale.reshape(nb, 1, d), norm_w.reshape(1, d), w_r)


def _moe_kernel(xs_ref, wg_ref, wu_ref, wd_ref, aff_ref, o_ref, acc):
    f = pl.program_id(2)
    x = xs_ref[0]
    g = _dot(x, wg_ref[0])
    u = _dot(x, wu_ref[0])
    hid = (g * _sigmoid(g) * u).astype(BF16)
    contrib = _dot(hid, wd_ref[0])

    @pl.when(f == 0)
    def _():
        acc[...] = contrib

    @pl.when(f > 0)
    def _():
        acc[...] += contrib

    @pl.when(f == pl.num_programs(2) - 1)
    def _():
        o_ref[0] = acc[...] * aff_ref[0]


def _moe_ffn(xs, w_gate, w_up, w_down, aff, tm, tf):
    e, m, d = xs.shape
    ff = w_gate.shape[2]
    return pl.pallas_call(
        _moe_kernel,
        grid=(e, m // tm, ff // tf),
        in_specs=[pl.BlockSpec((1, tm, d), lambda e_, i, f: (e_, i, 0)),
                  pl.BlockSpec((1, d, tf), lambda e_, i, f: (e_, 0, f)),
                  pl.BlockSpec((1, d, tf), lambda e_, i, f: (e_, 0, f)),
                  pl.BlockSpec((1, tf, d), lambda e_, i, f: (e_, f, 0)),
                  pl.BlockSpec((1, tm, 1), lambda e_, i, f: (e_, i, 0))],
        out_specs=pl.BlockSpec((1, tm, d), lambda e_, i, f: (e_, i, 0)),
        out_shape=jax.ShapeDtypeStruct((e, m, d), F32),
        scratch_shapes=[pltpu.VMEM((tm, d), F32)],
        compiler_params=_cparams(("parallel", "parallel", "arbitrary")),
        name="moe_ffn",
    )(xs, w_gate, w_up, w_down, aff)


def _final_kernel(x_ref, y_ref, gt_ref, nw_ref, o_ref):
    x2 = x_ref[...] + gt_ref[0] * y_ref[...]
    o_ref[...] = x2 * lax.rsqrt(jnp.mean(x2 * x2, axis=-1, keepdims=True) + NORM_EPS) * nw_ref[...]


def _final(x1, y, gate, norm_w, rows_per_mod, tm):
    m, d = x1.shape
    nb = gate.shape[0]
    return pl.pallas_call(
        _final_kernel,
        grid=(m // tm,),
        in_specs=[pl.BlockSpec((tm, d), lambda i: (i, 0)),
                  pl.BlockSpec((tm, d), lambda i: (i, 0)),
                  pl.BlockSpec((1, 1, d), lambda i: ((i * tm) // rows_per_mod, 0, 0)),
                  pl.BlockSpec((1, d), lambda i: (0, 0))],
        out_specs=pl.BlockSpec((tm, d), lambda i: (i, 0)),
        out_shape=jax.ShapeDtypeStruct((m, d), F32),
        compiler_params=_cparams(("parallel",)),
        name="final_norm",
    )(x1, y, gate.reshape(nb, 1, d), norm_w.reshape(1, d))


def kernel(x, c, ctx, c_ctx, w_ada, b_ada, norm1, norm2, w_in, conv_qkv, a_log, dt_bias, gn_w, w_dn_out,
           conv_sc, w_sc_out, w_o, w_router, w_gate, w_up, w_down, norm_f):
    bsz, seq, d = x.shape
    m = bsz * seq
    l = 0

    pad = (-(bsz + 1)) % SUBLANES
    c_all = jnp.concatenate([c, c_ctx[None, :], jnp.zeros((pad, d), F32)], axis=0)
    mod_all = _ada(c_all, w_ada[l], b_ada[l])
    sh1, sc1, gt1, sh2, sc2, gt2 = jnp.split(mod_all[:bsz], 6, axis=-1)
    csh1, csc1 = mod_all[bsz:bsz + 1, :d], mod_all[bsz:bsz + 1, d:2 * d]

    wi = w_in[l]
    w_main = jnp.concatenate([wi[:, :N_QKV], wi[:, N_STATE_COLS:]], axis=1).astype(BF16)
    w_ba = jnp.pad(wi[:, N_QKV:N_STATE_COLS], ((0, 0), (0, LANES - N_BA))).astype(BF16)
    w_ba_t = w_ba.T
    lane_par = lambda a: jnp.pad(a.reshape(-1), (2 * DN_HEADS, LANES - N_BA))
    alog, dtb = lane_par(a_log[l]), lane_par(dt_bias[l])

    x2d = x.reshape(m, d)
    p_main, gb_l, gbt_l = _in_proj(x2d, sh1, sc1, norm1[l], w_main, w_ba, w_ba_t, alog, dtb, N_MAIN, seq,
                                   1024, 512)
    pc_qkv, gb_c, gbt_c = _in_proj(ctx.reshape(bsz * CTX_LEN, d), csh1, csc1, norm1[l], w_main, w_ba, w_ba_t,
                                   alog, dtb, N_QKV, bsz * CTX_LEN, 1024, 512)

    a_in = _deltanet(pc_qkv, gb_c, gbt_c, p_main, gb_l, gbt_l, conv_qkv[l], gn_w[l], bsz)

    w_r = jnp.pad(w_router[l], ((0, 0), (0, LANES - N_EXPERTS))).astype(BF16)
    x1, h2, logits = _merge(a_in, p_main, conv_sc[l], w_dn_out[l].astype(BF16), w_sc_out[l].astype(BF16),
                            w_o[l].astype(BF16), x2d, gt1, sh2, sc2, norm2[l], w_r, seq, 512, 512)

    cap = CAPACITY_FACTOR * seq // N_EXPERTS
    aff = jax.nn.softmax(logits[:, :N_EXPERTS].reshape(bsz, seq, N_EXPERTS), axis=-1)
    top_aff, idx = lax.top_k(aff.transpose(0, 2, 1), cap)
    flat_idx = (idx + (jnp.arange(bsz, dtype=idx.dtype) * seq)[:, None, None]).transpose(1, 0, 2)
    flat_idx = flat_idx.reshape(N_EXPERTS, bsz * cap)
    xs = jnp.take(h2, flat_idx, axis=0)
    aff_e = top_aff.transpose(1, 0, 2).reshape(N_EXPERTS, bsz * cap, 1)
    ye = _moe_ffn(xs, w_gate[l].astype(BF16), w_up[l].astype(BF16), w_down[l].astype(BF16), aff_e, 1024, 512)
    y = jnp.zeros((m, d), F32).at[flat_idx.reshape(-1)].add(ye.reshape(-1, d))

    out = _final(x1, y, gt2, norm_f, seq, 512)
    return out.reshape(bsz, seq, d)
```

```python
import jax
import jax.numpy as jnp
from jax import lax
from jax.experimental import pallas as pl
from jax.experimental.pallas import tpu as pltpu

F32 = jnp.float32
BF16 = jnp.bfloat16

D_MODEL = 2048
CTX_LEN = 256
GRID_W = 64
DN_HEADS = 8
DN_HEAD_DIM = 128
DN_WIDTH = DN_HEADS * DN_HEAD_DIM
SC_WIDTH = 1024
N_EXPERTS = 16
EXPERT_FF = 2048
CAPACITY_FACTOR = 2
NORM_EPS = 1e-6
N_QKV = 3 * DN_WIDTH
N_BA = 4 * DN_HEADS
N_STATE_COLS = N_QKV + N_BA
LANES = 128
SUBLANES = 8
COL_Z = N_QKV
COL_SC = COL_Z + DN_WIDTH
COL_GA = COL_SC + 3 * SC_WIDTH
COL_GB = COL_GA + D_MODEL
N_MAIN = COL_GB + D_MODEL
DN_BLOCK = LANES
DN_INVERSE_LEVELS = 7
DN_PREP_GROUP = 4

VMEM_LIMIT = 56 * 1024 * 1024


def _cparams(sem):
    return pltpu.CompilerParams(dimension_semantics=sem, vmem_limit_bytes=VMEM_LIMIT)


def _dot(a, b):
    return jnp.dot(a, b, preferred_element_type=F32)


def _dot_nt(a, b):
    return lax.dot_general(a, b, (((1,), (1,)), ((), ())), preferred_element_type=F32)


def _sigmoid(x):
    return 1.0 / (1.0 + jnp.exp(-x))


def _softplus(x):
    return jnp.maximum(x, 0.0) + jnp.log(1.0 + jnp.exp(-jnp.abs(x)))


def _ada_kernel(c_ref, w_ref, b_ref, o_ref):
    c = c_ref[...]
    s = (c * _sigmoid(c)).astype(BF16)
    o_ref[...] = _dot(s, w_ref[...].astype(BF16)) + b_ref[...]


def _ada(c_all, w, b):
    rows, d = c_all.shape
    n = w.shape[1]
    tn = 1024
    return pl.pallas_call(
        _ada_kernel,
        grid=(n // tn,),
        in_specs=[pl.BlockSpec((rows, d), lambda j: (0, 0)),
                  pl.BlockSpec((d, tn), lambda j: (0, j)),
                  pl.BlockSpec((1, tn), lambda j: (0, j))],
        out_specs=pl.BlockSpec((rows, tn), lambda j: (0, j)),
        out_shape=jax.ShapeDtypeStruct((rows, n), F32),
        compiler_params=_cparams(("arbitrary",)),
        name="ada_ln",
    )(c_all, w, b.reshape(1, n))


def _gate_act(raw, idx, alog, dtb):
    return jnp.where(idx < 2 * DN_HEADS, _sigmoid(raw), -jnp.exp(alog) * _softplus(raw + dtb))


def _in_proj_kernel(x_ref, sh_ref, sc_ref, nw_ref, w_ref, wbat_ref, alogt_ref, dtbt_ref, o_ref, gbt_ref, h_scr):
    @pl.when(pl.program_id(1) == 0)
    def _():
        x = x_ref[...]
        y = x * lax.rsqrt(jnp.mean(x * x, axis=-1, keepdims=True) + NORM_EPS) * nw_ref[...]
        hb = (y * (1.0 + sc_ref[0]) + sh_ref[0]).astype(BF16)
        h_scr[...] = hb
        raw_t = _dot_nt(wbat_ref[...], hb)
        gbt_ref[...] = _gate_act(raw_t, lax.broadcasted_iota(jnp.int32, raw_t.shape, 0), alogt_ref[...],
                                 dtbt_ref[...])

    o_ref[...] = _dot(h_scr[...], w_ref[...]).astype(o_ref.dtype)


def _in_proj(x2d, shift, scale, norm_w, w_main, w_ba_t, alog, dtb, n_out, rows_per_mod, tm, tn):
    m, d = x2d.shape
    nb = shift.shape[0]
    mod_idx = lambda i, j: ((i * tm) // rows_per_mod, 0, 0)
    const = lambda i, j: (0, 0)
    return pl.pallas_call(
        _in_proj_kernel,
        grid=(m // tm, n_out // tn),
        in_specs=[pl.BlockSpec((tm, d), lambda i, j: (i, 0)),
                  pl.BlockSpec((1, 1, d), mod_idx),
                  pl.BlockSpec((1, 1, d), mod_idx),
                  pl.BlockSpec((1, d), const),
                  pl.BlockSpec((d, tn), lambda i, j: (0, j)),
                  pl.BlockSpec((LANES, d), const),
                  pl.BlockSpec((LANES, 1), const),
                  pl.BlockSpec((LANES, 1), const)],
        out_specs=[pl.BlockSpec((tm, tn), lambda i, j: (i, j)),
                   pl.BlockSpec((LANES, tm), lambda i, j: (0, i))],
        out_shape=[jax.ShapeDtypeStruct((m, n_out), BF16),
                   jax.ShapeDtypeStruct((LANES, m), F32)],
        scratch_shapes=[pltpu.VMEM((tm, d), BF16)],
        compiler_params=_cparams(("parallel", "arbitrary")),
        name="in_proj",
    )(x2d, shift.reshape(nb, 1, d), scale.reshape(nb, 1, d), norm_w.reshape(1, d), w_main, w_ba_t,
      alog.reshape(LANES, 1), dtb.reshape(LANES, 1))


def _conv_silu(x, w, t):
    row = lax.broadcasted_iota(jnp.int32, x.shape, 0)
    prev = jnp.where(row == 0, 0.0, pltpu.roll(x, 1, 0))
    nxt = jnp.where(row == t - 1, 0.0, pltpu.roll(x, t - 1, 0))
    y = w[0:1, :] * prev + w[1:2, :] * x + w[2:3, :] * nxt
    return y * _sigmoid(y)


def _chunk_cumsum(x, axis, reverse):
    n = x.shape[axis]
    pos = lax.broadcasted_iota(jnp.int32, x.shape, axis) % DN_BLOCK
    s = 1
    while s < DN_BLOCK:
        if reverse:
            x = x + jnp.where(pos + s < DN_BLOCK, pltpu.roll(x, n - s, axis), 0.0)
        else:
            x = x + jnp.where(pos >= s, pltpu.roll(x, s, axis), 0.0)
        s *= 2
    return x


def _dn_load(t, h, q_ref, k_ref, v_ref, gbt_ref, cwq_ref, cwk_ref, cwv_ref, q_s, k_s, v_s, bfrow_s, bbrow_s, gfrow_s,
             grrow_s):
    q = _conv_silu(q_ref[...].astype(F32), cwq_ref[...], t)
    q_s[0:t, :] = q * lax.rsqrt(jnp.sum(q * q, axis=-1, keepdims=True) + NORM_EPS) * (DN_HEAD_DIM ** -0.5)
    k = _conv_silu(k_ref[...].astype(F32), cwk_ref[...], t)
    k_s[0:t, :] = k * lax.rsqrt(jnp.sum(k * k, axis=-1, keepdims=True) + NORM_EPS)
    v_s[0:t, :] = _conv_silu(v_ref[...].astype(F32), cwv_ref[...], t)
    for dst, base, reverse in ((bfrow_s, 0, None), (bbrow_s, DN_HEADS, None), (gfrow_s, 2 * DN_HEADS, False),
                               (grrow_s, 3 * DN_HEADS, True)):
        grp = gbt_ref[base:base + DN_HEADS, :]
        sub = lax.broadcasted_iota(jnp.int32, grp.shape, 0)
        val = jnp.broadcast_to(jnp.sum(jnp.where(sub == h, grp, 0.0), axis=0, keepdims=True), grp.shape)
        dst[:, 0:t] = val if reverse is None else _chunk_cumsum(val, 1, reverse)


def _dn_group_prep(chunks, need_o, q_s, k_s, v_s, bfrow_s, bbrow_s, gfrow_s, grrow_s, fwd_s, bwd_s):
    L = DN_BLOCK
    ri = lax.broadcasted_iota(jnp.int32, (L, L), 0)
    ci = lax.broadcasted_iota(jnp.int32, (L, L), 1)
    blk = ri ^ ci
    eye = jnp.where(ri == ci, 1.0, 0.0)
    chains = []
    for c in chunks:
        r0 = pl.multiple_of(c * L, L)
        rows = pl.ds(r0, L)
        q, k, v = q_s[rows, :], k_s[rows, :], v_s[rows, :]
        k16 = k.astype(BF16)
        kk = _dot_nt(k16, k16)
        qk = _dot_nt(q.astype(BF16), k16) if need_o else None
        for reverse, brow_s, grow_s, outs in ((False, bfrow_s, gfrow_s, fwd_s), (True, bbrow_s, grrow_s, bwd_s)):
            gc_row = jnp.broadcast_to(grow_s[0:1, pl.ds(r0, L)], (L, L))
            gc_col = jnp.transpose(gc_row)
            beta = jnp.transpose(jnp.broadcast_to(brow_s[0:1, pl.ds(r0, L)], (L, L)))
            incl, strict = (ri <= ci, ri < ci) if reverse else (ri >= ci, ri > ci)
            decay = jnp.exp(jnp.where(incl, gc_col - gc_row, -jnp.inf))
            a_tri = jnp.where(strict, beta * kk * decay, 0.0)
            g_last = gc_col[0:1, :] if reverse else gc_col[L - 1:L, :]
            chains.append(dict(c=c, rows=rows, q=q, k=k, v=v, qk=qk, beta=beta, gc_col=gc_col, decay=decay,
                               a_tri=a_tri, g_last=g_last, t_inv=eye, outs=outs))
    for level in range(DN_INVERSE_LEVELS):
        for ch in chains:
            ch["t16"] = ch["t_inv"].astype(BF16)
            a_off = jnp.where((blk >> level) == 1, ch["a_tri"], 0.0).astype(BF16)
            ch["m"] = _dot(a_off, ch["t16"]).astype(BF16)
        for ch in chains:
            ch["t_inv"] = ch["t_inv"] - _dot(ch["t16"], ch["m"])
    for ch in chains:
        egc = jnp.exp(ch["gc_col"])
        rhs = jnp.concatenate([ch["v"] * ch["beta"], ch["k"] * ch["beta"] * egc], axis=-1).astype(BF16)
        ch["sol"] = _dot(ch["t_inv"].astype(BF16), rhs)
        ch["egc"] = egc
    for ch in chains:
        u_s, w_s, qk_s, qd_s, kdt_s, egl_s = ch["outs"]
        rows, c = ch["rows"], ch["c"]
        u_s[rows, :] = ch["sol"][:, :DN_HEAD_DIM]
        w_s[rows, :] = ch["sol"][:, DN_HEAD_DIM:].astype(BF16)
        kd = ch["k"] * jnp.exp(ch["g_last"] - ch["gc_col"])
        kdt_s[rows, :] = jnp.transpose(kd).astype(BF16)
        egl_s[pl.ds(pl.multiple_of(c * SUBLANES, SUBLANES), SUBLANES), :] = jnp.broadcast_to(
            jnp.exp(ch["g_last"]), (SUBLANES, LANES))
        if need_o:
            qk_s[rows, :] = (ch["qk"] * ch["decay"]).astype(BF16)
            qd_s[rows, :] = (ch["q"] * ch["egc"]).astype(BF16)


def _dn_chunk_step(c, s, need_o, u_s, w_s, qk_s, qd_s, kdt_s, egl_s, o_s):
    L = DN_BLOCK
    rows = pl.ds(pl.multiple_of(c * L, L), L)
    s16 = s.astype(BF16)
    v_new = (u_s[rows, :] - _dot(w_s[rows, :], s16)).astype(BF16)
    if need_o:
        o_s[rows, :] += _dot(qd_s[rows, :], s16) + _dot(qk_s[rows, :], v_new)
    egl = egl_s[pl.ds(pl.multiple_of(c * SUBLANES, SUBLANES), 1), :]
    return s * egl + _dot(kdt_s[rows, :], v_new)


def _dn_scan(n, need_o, state, q_s, k_s, v_s, bfrow_s, bbrow_s, gfrow_s, grrow_s, fwd_s, bwd_s, o_s):
    group = min(n, DN_PREP_GROUP)

    def prep(i, carry):
        _dn_group_prep([i * group + j for j in range(group)], need_o, q_s, k_s, v_s, bfrow_s, bbrow_s, gfrow_s,
                       grrow_s, fwd_s, bwd_s)
        return carry

    lax.fori_loop(0, n // group, prep, 0)

    def step(c, carry):
        s_f, s_b = carry
        s_f = _dn_chunk_step(c, s_f, need_o, *fwd_s, o_s)
        s_b = _dn_chunk_step(n - 1 - c, s_b, need_o, *bwd_s, o_s)
        return s_f, s_b

    return lax.fori_loop(0, n, step, state)


def _deltanet_kernel(qc_ref, kc_ref, vc_ref, gbtc_ref, ql_ref, kl_ref, vl_ref, zl_ref, gbtl_ref,
                     cwq_ref, cwk_ref, cwv_ref, gnw_ref, out_ref,
                     q_s, k_s, v_s, bfrow_s, bbrow_s, gfrow_s, grrow_s,
                     uf_s, wf_s, qkf_s, qdf_s, kdtf_s, eglf_s, ub_s, wb_s, qkb_s, qdb_s, kdtb_s, eglb_s, o_s):
    h = pl.program_id(1)
    t_ctx, t_lat = qc_ref.shape[0], ql_ref.shape[0]
    vec_s = (q_s, k_s, v_s, bfrow_s, bbrow_s, gfrow_s, grrow_s)
    fwd_s = (uf_s, wf_s, qkf_s, qdf_s, kdtf_s, eglf_s)
    bwd_s = (ub_s, wb_s, qkb_s, qdb_s, kdtb_s, eglb_s)
    zero = jnp.zeros((DN_HEAD_DIM, DN_HEAD_DIM), F32)

    _dn_load(t_ctx, h, qc_ref, kc_ref, vc_ref, gbtc_ref, cwq_ref, cwk_ref, cwv_ref, *vec_s)
    state = _dn_scan(t_ctx // DN_BLOCK, False, (zero, zero), *vec_s, fwd_s, bwd_s, o_s)

    _dn_load(t_lat, h, ql_ref, kl_ref, vl_ref, gbtl_ref, cwq_ref, cwk_ref, cwv_ref, *vec_s)
    o_s[...] = jnp.zeros_like(o_s)
    _dn_scan(t_lat // DN_BLOCK, True, state, *vec_s, fwd_s, bwd_s, o_s)

    o = o_s[...]
    z = zl_ref[...].astype(F32)
    o = o * lax.rsqrt(jnp.mean(o * o, axis=-1, keepdims=True) + NORM_EPS) * gnw_ref[...]
    out_ref[...] = (o * (z * _sigmoid(z))).astype(out_ref.dtype)


def _deltanet(pc_qkv, gbt_c, p_main, gbt_l, conv_w, gn_w, bsz):
    t_ctx = pc_qkv.shape[0] // bsz
    t_lat = p_main.shape[0] // bsz
    hd, nh = DN_HEAD_DIM, DN_HEADS
    col = lambda off: (lambda b, h: (b, off + h))
    col0 = lambda b, h: (0, b)
    wcol = lambda off: (lambda b, h: (0, off + h))
    vec = lambda dt: pltpu.VMEM((t_lat, hd), dt)
    rowvec = pltpu.VMEM((SUBLANES, t_lat), F32)
    per_dir = [vec(F32), vec(BF16), vec(BF16), vec(BF16), vec(BF16),
               pltpu.VMEM((t_lat // DN_BLOCK * SUBLANES, LANES), F32)]
    return pl.pallas_call(
        _deltanet_kernel,
        grid=(bsz, nh),
        in_specs=[pl.BlockSpec((t_ctx, hd), col(0)), pl.BlockSpec((t_ctx, hd), col(nh)),
                  pl.BlockSpec((t_ctx, hd), col(2 * nh)), pl.BlockSpec((LANES, t_ctx), col0),
                  pl.BlockSpec((t_lat, hd), col(0)), pl.BlockSpec((t_lat, hd), col(nh)),
                  pl.BlockSpec((t_lat, hd), col(2 * nh)), pl.BlockSpec((t_lat, hd), col(3 * nh)),
                  pl.BlockSpec((LANES, t_lat), col0),
                  pl.BlockSpec((3, hd), wcol(0)), pl.BlockSpec((3, hd), wcol(nh)),
                  pl.BlockSpec((3, hd), wcol(2 * nh)),
                  pl.BlockSpec((1, hd), lambda b, h: (0, 0))],
        out_specs=pl.BlockSpec((t_lat, hd), lambda b, h: (b, h)),
        out_shape=jax.ShapeDtypeStruct((p_main.shape[0], DN_WIDTH), BF16),
        scratch_shapes=[vec(F32)] * 3 + [rowvec] * 4 + per_dir + per_dir + [vec(F32)],
        compiler_params=_cparams(("parallel", "arbitrary")),
        name="deltanet",
    )(pc_qkv, pc_qkv, pc_qkv, gbt_c, p_main, p_main, p_main, p_main, gbt_l,
      conv_w, conv_w, conv_w, gn_w.reshape(1, hd))


def _merge_kernel(a_ref, bg_ref, cg_ref, xin_ref, cw_ref, ga_ref, gb_ref, wdn_ref, wsc_ref, wo_ref, x_ref, gt_ref,
                  sh_ref, sc_ref, nw_ref, wr_ref, x1_ref, h2_ref, lg_ref, acc, b_scr):
    j = pl.program_id(1)

    @pl.when(j == 0)
    def _():
        zc = cg_ref[...].astype(F32) * xin_ref[...].astype(F32)
        tm = zc.shape[0]
        pos = lax.broadcasted_iota(jnp.int32, zc.shape, 0) % GRID_W
        prev = jnp.where(pos == 0, 0.0, pltpu.roll(zc, 1, 0))
        nxt = jnp.where(pos == GRID_W - 1, 0.0, pltpu.roll(zc, tm - 1, 0))
        cw = cw_ref[...]
        conv = cw[0:1, :] * prev + cw[1:2, :] * zc + cw[2:3, :] * nxt
        b_scr[...] = (bg_ref[...].astype(F32) * conv).astype(BF16)

    ya = _dot(a_ref[...], wdn_ref[...])
    yb = _dot(b_scr[...], wsc_ref[...])
    mix = _sigmoid(ga_ref[...].astype(F32)) * ya + _sigmoid(gb_ref[...].astype(F32)) * yb
    contrib = _dot(mix.astype(BF16), wo_ref[...])

    @pl.when(j == 0)
    def _():
        acc[...] = contrib

    @pl.when(j > 0)
    def _():
        acc[...] += contrib

    @pl.when(j == pl.num_programs(1) - 1)
    def _():
        x1 = x_ref[...] + gt_ref[0] * acc[...]
        x1_ref[...] = x1
        y = x1 * lax.rsqrt(jnp.mean(x1 * x1, axis=-1, keepdims=True) + NORM_EPS) * nw_ref[...]
        hb = (y * (1.0 + sc_ref[0]) + sh_ref[0]).astype(BF16)
        h2_ref[...] = hb
        lg_ref[...] = _dot(hb, wr_ref[...])


def _merge(a_in, p_main, conv_sc, w_dn, w_sc, w_o, x2d, gate, shift, scale, norm_w, w_r, rows_per_mod, tm, tj):
    m, d = x2d.shape
    nb = gate.shape[0]
    ga0, gb0 = COL_GA // tj, COL_GB // tj
    sc0 = COL_SC // SC_WIDTH
    mod_idx = lambda i, j: ((i * tm) // rows_per_mod, 0, 0)
    const = lambda i, j: (0, 0)
    rowblk = lambda i, j: (i, 0)
    return pl.pallas_call(
        _merge_kernel,
        grid=(m // tm, d // tj),
        in_specs=[pl.BlockSpec((tm, DN_WIDTH), rowblk),
                  pl.BlockSpec((tm, SC_WIDTH), lambda i, j: (i, sc0)),
                  pl.BlockSpec((tm, SC_WIDTH), lambda i, j: (i, sc0 + 1)),
                  pl.BlockSpec((tm, SC_WIDTH), lambda i, j: (i, sc0 + 2)),
                  pl.BlockSpec((3, SC_WIDTH), const),
                  pl.BlockSpec((tm, tj), lambda i, j: (i, ga0 + j)),
                  pl.BlockSpec((tm, tj), lambda i, j: (i, gb0 + j)),
                  pl.BlockSpec((DN_WIDTH, tj), lambda i, j: (0, j)),
                  pl.BlockSpec((SC_WIDTH, tj), lambda i, j: (0, j)),
                  pl.BlockSpec((tj, d), lambda i, j: (j, 0)),
                  pl.BlockSpec((tm, d), rowblk),
                  pl.BlockSpec((1, 1, d), mod_idx),
                  pl.BlockSpec((1, 1, d), mod_idx),
                  pl.BlockSpec((1, 1, d), mod_idx),
                  pl.BlockSpec((1, d), const),
                  pl.BlockSpec((d, LANES), const)],
        out_specs=[pl.BlockSpec((tm, d), rowblk),
                   pl.BlockSpec((tm, d), rowblk),
                   pl.BlockSpec((tm, LANES), rowblk)],
        out_shape=[jax.ShapeDtypeStruct((m, d), F32),
                   jax.ShapeDtypeStruct((m, d), BF16),
                   jax.ShapeDtypeStruct((m, LANES), F32)],
        scratch_shapes=[pltpu.VMEM((tm, d), F32), pltpu.VMEM((tm, SC_WIDTH), BF16)],
        compiler_params=_cparams(("parallel", "arbitrary")),
        name="merge_out",
    )(a_in, p_main, p_main, p_main, conv_sc, p_main, p_main, w_dn, w_sc, w_o, x2d, gate.reshape(nb, 1, d),
      shift.reshape(nb, 1, d), scale.reshape(nb, 1, d), norm_w.reshape(1, d), w_r)


def _moe_kernel(xs_ref, wg_ref, wu_ref, wd_ref, aff_ref, o_ref, acc):
    f = pl.program_id(2)
    x = xs_ref[0]
    g = _dot(x, wg_ref[0])
    u = _dot(x, wu_ref[0])
    hid = (g * _sigmoid(g) * u).astype(BF16)
    contrib = _dot(hid, wd_ref[0])

    @pl.when(f == 0)
    def _():
        acc[...] = contrib

    @pl.when(f > 0)
    def _():
        acc[...] += contrib

    @pl.when(f == pl.num_programs(2) - 1)
    def _():
        o_ref[0] = (acc[...] * aff_ref[0]).astype(o_ref.dtype)


def _moe_ffn(xs, w_gate, w_up, w_down, aff, tm, tf):
    e, m, d = xs.shape
    ff = w_gate.shape[2]
    return pl.pallas_call(
        _moe_kernel,
        grid=(e, m // tm, ff // tf),
        in_specs=[pl.BlockSpec((1, tm, d), lambda e_, i, f: (e_, i, 0)),
                  pl.BlockSpec((1, d, tf), lambda e_, i, f: (e_, 0, f)),
                  pl.BlockSpec((1, d, tf), lambda e_, i, f: (e_, 0, f)),
                  pl.BlockSpec((1, tf, d), lambda e_, i, f: (e_, f, 0)),
                  pl.BlockSpec((1, tm, 1), lambda e_, i, f: (e_, i, 0))],
        out_specs=pl.BlockSpec((1, tm, d), lambda e_, i, f: (e_, i, 0)),
        out_shape=jax.ShapeDtypeStruct((e, m, d), BF16),
        scratch_shapes=[pltpu.VMEM((tm, d), F32)],
        compiler_params=_cparams(("parallel", "parallel", "arbitrary")),
        name="moe_ffn",
    )(xs, w_gate, w_up, w_down, aff)


def _combine_kernel(idx_ref, ye_ref, x_ref, gt_ref, nw_ref, o_ref, acc):
    g = pl.program_id(2)
    tt, d = acc.shape
    idx = idx_ref[0, 0]
    tok = lax.broadcasted_iota(jnp.int32, (tt, idx.shape[-1]), 0) + pl.program_id(1) * tt
    onehot = jnp.where(tok == idx, 1.0, 0.0).astype(BF16)
    contrib = _dot(onehot, ye_ref[...].reshape(idx.shape[-1], d))

    @pl.when(g == 0)
    def _():
        acc[...] = contrib

    @pl.when(g > 0)
    def _():
        acc[...] += contrib

    @pl.when(g == pl.num_programs(2) - 1)
    def _():
        x2 = x_ref[...] + gt_ref[0] * acc[...]
        o_ref[...] = x2 * lax.rsqrt(jnp.mean(x2 * x2, axis=-1, keepdims=True) + NORM_EPS) * nw_ref[...]


def _combine(idx, ye, x1, gate, norm_w, seq, tt, eg):
    bsz, ne, cap = idx.shape
    m, d = x1.shape
    tiles = seq // tt
    return pl.pallas_call(
        _combine_kernel,
        grid=(bsz, tiles, ne // eg),
        in_specs=[pl.BlockSpec((1, 1, 1, eg * cap), lambda b, t, g: (b, g, 0, 0)),
                  pl.BlockSpec((eg, cap, d), lambda b, t, g: (g, b, 0)),
                  pl.BlockSpec((tt, d), lambda b, t, g: (b * tiles + t, 0)),
                  pl.BlockSpec((1, 1, d), lambda b, t, g: (b, 0, 0)),
                  pl.BlockSpec((1, d), lambda b, t, g: (0, 0))],
        out_specs=pl.BlockSpec((tt, d), lambda b, t, g: (b * tiles + t, 0)),
        out_shape=jax.ShapeDtypeStruct((m, d), F32),
        scratch_shapes=[pltpu.VMEM((tt, d), F32)],
        compiler_params=_cparams(("parallel", "parallel", "arbitrary")),
        name="combine_final",
    )(idx.reshape(bsz, ne // eg, 1, eg * cap), ye, x1, gate.reshape(bsz, 1, d), norm_w.reshape(1, d))


def kernel(x, c, ctx, c_ctx, w_ada, b_ada, norm1, norm2, w_in, conv_qkv, a_log, dt_bias, gn_w, w_dn_out,
           conv_sc, w_sc_out, w_o, w_router, w_gate, w_up, w_down, norm_f):
    bsz, seq, d = x.shape
    m = bsz * seq
    l = 0

    pad = (-(bsz + 1)) % SUBLANES
    c_all = jnp.concatenate([c, c_ctx[None, :], jnp.zeros((pad, d), F32)], axis=0)
    mod_all = _ada(c_all, w_ada[l], b_ada[l])
    sh1, sc1, gt1, sh2, sc2, gt2 = jnp.split(mod_all[:bsz], 6, axis=-1)
    csh1, csc1 = mod_all[bsz:bsz + 1, :d], mod_all[bsz:bsz + 1, d:2 * d]

    wi = w_in[l]
    w_main = jnp.concatenate([wi[:, :N_QKV], wi[:, N_STATE_COLS:]], axis=1).astype(BF16)
    w_ba_t = jnp.pad(wi[:, N_QKV:N_STATE_COLS], ((0, 0), (0, LANES - N_BA))).astype(BF16).T
    lane_par = lambda a: jnp.pad(a.reshape(-1), (2 * DN_HEADS, LANES - N_BA))
    alog, dtb = lane_par(a_log[l]), lane_par(dt_bias[l])

    x2d = x.reshape(m, d)
    p_main, gbt_l = _in_proj(x2d, sh1, sc1, norm1[l], w_main, w_ba_t, alog, dtb, N_MAIN, seq, 1024, 512)
    pc_qkv, gbt_c = _in_proj(ctx.reshape(bsz * CTX_LEN, d), csh1, csc1, norm1[l], w_main, w_ba_t, alog, dtb,
                             N_QKV, bsz * CTX_LEN, 1024, 512)

    a_in = _deltanet(pc_qkv, gbt_c, p_main, gbt_l, conv_qkv[l], gn_w[l], bsz)

    w_r = jnp.pad(w_router[l], ((0, 0), (0, LANES - N_EXPERTS))).astype(BF16)
    x1, h2, logits = _merge(a_in, p_main, conv_sc[l], w_dn_out[l].astype(BF16), w_sc_out[l].astype(BF16),
                            w_o[l].astype(BF16), x2d, gt1, sh2, sc2, norm2[l], w_r, seq, 512, 512)

    cap = CAPACITY_FACTOR * seq // N_EXPERTS
    aff = jax.nn.softmax(logits[:, :N_EXPERTS].reshape(bsz, seq, N_EXPERTS), axis=-1)
    top_aff, idx = lax.top_k(aff.transpose(0, 2, 1), cap)
    flat_idx = (idx + (jnp.arange(bsz, dtype=idx.dtype) * seq)[:, None, None]).transpose(1, 0, 2)
    flat_idx = flat_idx.reshape(N_EXPERTS, bsz * cap)
    xs = jnp.take(h2, flat_idx, axis=0)
    aff_e = top_aff.transpose(1, 0, 2).reshape(N_EXPERTS, bsz * cap, 1)
    ye = _moe_ffn(xs, w_gate[l].astype(BF16), w_up[l].astype(BF16), w_down[l].astype(BF16), aff_e, 1024, 512)
    out = _combine(idx, ye, x1, gt2, norm_f, seq, 512, 4)
    return out.reshape(bsz, seq, d)
```

```python
import jax
import jax.numpy as jnp
from jax import lax
from jax.experimental import pallas as pl
from jax.experimental.pallas import tpu as pltpu

F32 = jnp.float32
BF16 = jnp.bfloat16

D_MODEL = 2048
CTX_LEN = 256
GRID_W = 64
DN_HEADS = 8
DN_HEAD_DIM = 128
DN_WIDTH = DN_HEADS * DN_HEAD_DIM
SC_WIDTH = 1024
N_EXPERTS = 16
EXPERT_FF = 2048
CAPACITY_FACTOR = 2
NORM_EPS = 1e-6
N_QKV = 3 * DN_WIDTH
N_BA = 4 * DN_HEADS
N_STATE_COLS = N_QKV + N_BA
LANES = 128
SUBLANES = 8
COL_Z = N_QKV
COL_SC = COL_Z + DN_WIDTH
COL_GA = COL_SC + 3 * SC_WIDTH
COL_GB = COL_GA + D_MODEL
N_MAIN = COL_GB + D_MODEL
DN_BLOCK = LANES
DN_INVERSE_LEVELS = 7
DN_PREP_GROUP = 4

VMEM_LIMIT = 56 * 1024 * 1024


def _cparams(sem):
    return pltpu.CompilerParams(dimension_semantics=sem, vmem_limit_bytes=VMEM_LIMIT)


def _dot(a, b):
    return jnp.dot(a, b, preferred_element_type=F32)


def _dot_nt(a, b):
    return lax.dot_general(a, b, (((1,), (1,)), ((), ())), preferred_element_type=F32)


def _sigmoid(x):
    return 0.5 * jnp.tanh(0.5 * x) + 0.5


def _softplus(x):
    return jnp.maximum(x, 0.0) + jnp.log(1.0 + jnp.exp(-jnp.abs(x)))


def _ada_kernel(c_ref, w_ref, b_ref, o_ref):
    c = c_ref[...]
    s = (c * _sigmoid(c)).astype(BF16)
    o_ref[...] = _dot(s, w_ref[...].astype(BF16)) + b_ref[...]


def _ada(c_all, w, b):
    rows, d = c_all.shape
    n = w.shape[1]
    tn = 1024
    return pl.pallas_call(
        _ada_kernel,
        grid=(n // tn,),
        in_specs=[pl.BlockSpec((rows, d), lambda j: (0, 0)),
                  pl.BlockSpec((d, tn), lambda j: (0, j)),
                  pl.BlockSpec((1, tn), lambda j: (0, j))],
        out_specs=pl.BlockSpec((rows, tn), lambda j: (0, j)),
        out_shape=jax.ShapeDtypeStruct((rows, n), F32),
        compiler_params=_cparams(("arbitrary",)),
        name="ada_ln",
    )(c_all, w, b.reshape(1, n))


def _gate_act(raw, idx, alog, dtb):
    return jnp.where(idx < 2 * DN_HEADS, _sigmoid(raw), -jnp.exp(alog) * _softplus(raw + dtb))


def _in_proj_kernel(x_ref, sh_ref, sc_ref, nw_ref, w_ref, wbat_ref, alogt_ref, dtbt_ref, o_ref, gbt_ref, h_scr):
    @pl.when(pl.program_id(1) == 0)
    def _():
        x = x_ref[...]
        y = x * lax.rsqrt(jnp.mean(x * x, axis=-1, keepdims=True) + NORM_EPS) * nw_ref[...]
        hb = (y * (1.0 + sc_ref[0]) + sh_ref[0]).astype(BF16)
        h_scr[...] = hb
        raw_t = _dot_nt(wbat_ref[...], hb)
        gbt_ref[...] = _gate_act(raw_t, lax.broadcasted_iota(jnp.int32, raw_t.shape, 0), alogt_ref[...],
                                 dtbt_ref[...])

    o_ref[...] = _dot(h_scr[...], w_ref[...]).astype(o_ref.dtype)


def _in_proj(x2d, shift, scale, norm_w, w_main, w_ba_t, alog, dtb, n_out, rows_per_mod, tm, tn):
    m, d = x2d.shape
    nb = shift.shape[0]
    mod_idx = lambda i, j: ((i * tm) // rows_per_mod, 0, 0)
    const = lambda i, j: (0, 0)
    return pl.pallas_call(
        _in_proj_kernel,
        grid=(m // tm, n_out // tn),
        in_specs=[pl.BlockSpec((tm, d), lambda i, j: (i, 0)),
                  pl.BlockSpec((1, 1, d), mod_idx),
                  pl.BlockSpec((1, 1, d), mod_idx),
                  pl.BlockSpec((1, d), const),
                  pl.BlockSpec((d, tn), lambda i, j: (0, j)),
                  pl.BlockSpec((LANES, d), const),
                  pl.BlockSpec((LANES, 1), const),
                  pl.BlockSpec((LANES, 1), const)],
        out_specs=[pl.BlockSpec((tm, tn), lambda i, j: (i, j)),
                   pl.BlockSpec((LANES, tm), lambda i, j: (0, i))],
        out_shape=[jax.ShapeDtypeStruct((m, n_out), BF16),
                   jax.ShapeDtypeStruct((LANES, m), F32)],
        scratch_shapes=[pltpu.VMEM((tm, d), BF16)],
        compiler_params=_cparams(("parallel", "arbitrary")),
        name="in_proj",
    )(x2d, shift.reshape(nb, 1, d), scale.reshape(nb, 1, d), norm_w.reshape(1, d), w_main, w_ba_t,
      alog.reshape(LANES, 1), dtb.reshape(LANES, 1))


def _conv_silu(x, w, t):
    row = lax.broadcasted_iota(jnp.int32, x.shape, 0)
    prev = jnp.where(row == 0, 0.0, pltpu.roll(x, 1, 0))
    nxt = jnp.where(row == t - 1, 0.0, pltpu.roll(x, t - 1, 0))
    y = w[0:1, :] * prev + w[1:2, :] * x + w[2:3, :] * nxt
    return y * _sigmoid(y)


def _chunk_cumsum(x, axis, reverse):
    n = x.shape[axis]
    pos = lax.broadcasted_iota(jnp.int32, x.shape, axis) % DN_BLOCK
    s = 1
    while s < DN_BLOCK:
        if reverse:
            x = x + jnp.where(pos + s < DN_BLOCK, pltpu.roll(x, n - s, axis), 0.0)
        else:
            x = x + jnp.where(pos >= s, pltpu.roll(x, s, axis), 0.0)
        s *= 2
    return x


def _dn_load(t, h, q_ref, k_ref, v_ref, gbt_ref, cwq_ref, cwk_ref, cwv_ref, q_s, k_s, v_s, bfrow_s, bbrow_s, gfrow_s,
             grrow_s):
    q = _conv_silu(q_ref[...].astype(F32), cwq_ref[...], t)
    q_s[0:t, :] = q * lax.rsqrt(jnp.sum(q * q, axis=-1, keepdims=True) + NORM_EPS) * (DN_HEAD_DIM ** -0.5)
    k = _conv_silu(k_ref[...].astype(F32), cwk_ref[...], t)
    k_s[0:t, :] = k * lax.rsqrt(jnp.sum(k * k, axis=-1, keepdims=True) + NORM_EPS)
    v_s[0:t, :] = _conv_silu(v_ref[...].astype(F32), cwv_ref[...], t)
    for dst, base, reverse in ((bfrow_s, 0, None), (bbrow_s, DN_HEADS, None), (gfrow_s, 2 * DN_HEADS, False),
                               (grrow_s, 3 * DN_HEADS, True)):
        grp = gbt_ref[base:base + DN_HEADS, :]
        sub = lax.broadcasted_iota(jnp.int32, grp.shape, 0)
        val = jnp.broadcast_to(jnp.sum(jnp.where(sub == h, grp, 0.0), axis=0, keepdims=True), grp.shape)
        dst[:, 0:t] = val if reverse is None else _chunk_cumsum(val, 1, reverse)


def _dn_group_prep(chunks, need_o, q_s, k_s, v_s, bfrow_s, bbrow_s, gfrow_s, grrow_s, fwd_s, bwd_s):
    L = DN_BLOCK
    ri = lax.broadcasted_iota(jnp.int32, (L, L), 0)
    ci = lax.broadcasted_iota(jnp.int32, (L, L), 1)
    blk = ri ^ ci
    eye = jnp.where(ri == ci, 1.0, 0.0)
    chains = []
    for c in chunks:
        r0 = pl.multiple_of(c * L, L)
        rows = pl.ds(r0, L)
        q, k, v = q_s[rows, :], k_s[rows, :], v_s[rows, :]
        k16 = k.astype(BF16)
        kk = _dot_nt(k16, k16)
        qk = _dot_nt(q.astype(BF16), k16) if need_o else None
        for reverse, brow_s, grow_s, outs in ((False, bfrow_s, gfrow_s, fwd_s), (True, bbrow_s, grrow_s, bwd_s)):
            gc_row = jnp.broadcast_to(grow_s[0:1, pl.ds(r0, L)], (L, L))
            gc_col = jnp.transpose(gc_row)
            beta = jnp.transpose(jnp.broadcast_to(brow_s[0:1, pl.ds(r0, L)], (L, L)))
            incl, strict = (ri <= ci, ri < ci) if reverse else (ri >= ci, ri > ci)
            decay = jnp.exp(jnp.where(incl, gc_col - gc_row, -jnp.inf))
            a_tri = jnp.where(strict, beta * kk * decay, 0.0)
            g_last = gc_col[0:1, :] if reverse else gc_col[L - 1:L, :]
            chains.append(dict(c=c, rows=rows, q=q, k=k, v=v, qk=qk, beta=beta, gc_col=gc_col, decay=decay,
                               a_tri=a_tri, g_last=g_last, t_inv=eye, outs=outs))
    for level in range(DN_INVERSE_LEVELS):
        for ch in chains:
            ch["t16"] = ch["t_inv"].astype(BF16)
            a_off = jnp.where((blk >> level) == 1, ch["a_tri"], 0.0).astype(BF16)
            ch["m"] = _dot(a_off, ch["t16"]).astype(BF16)
        for ch in chains:
            ch["t_inv"] = ch["t_inv"] - _dot(ch["t16"], ch["m"])
    for ch in chains:
        egc = jnp.exp(ch["gc_col"])
        rhs = jnp.concatenate([ch["v"] * ch["beta"], ch["k"] * ch["beta"] * egc], axis=-1).astype(BF16)
        ch["sol"] = _dot(ch["t_inv"].astype(BF16), rhs)
        ch["egc"] = egc
    for ch in chains:
        u_s, w_s, qk_s, qd_s, kdt_s, egl_s = ch["outs"]
        rows, c = ch["rows"], ch["c"]
        u_s[rows, :] = ch["sol"][:, :DN_HEAD_DIM]
        w_s[rows, :] = ch["sol"][:, DN_HEAD_DIM:].astype(BF16)
        kd = ch["k"] * jnp.exp(ch["g_last"] - ch["gc_col"])
        kdt_s[rows, :] = jnp.transpose(kd).astype(BF16)
        egl_s[pl.ds(pl.multiple_of(c * SUBLANES, SUBLANES), SUBLANES), :] = jnp.broadcast_to(
            jnp.exp(ch["g_last"]), (SUBLANES, LANES))
        if need_o:
            qk_s[rows, :] = (ch["qk"] * ch["decay"]).astype(BF16)
            qd_s[rows, :] = (ch["q"] * ch["egc"]).astype(BF16)


def _dn_chunk_step(c, s, need_o, u_s, w_s, qk_s, qd_s, kdt_s, egl_s, o_s):
    L = DN_BLOCK
    rows = pl.ds(pl.multiple_of(c * L, L), L)
    s16 = s.astype(BF16)
    v_new = (u_s[rows, :] - _dot(w_s[rows, :], s16)).astype(BF16)
    if need_o:
        o_s[rows, :] += _dot(qd_s[rows, :], s16) + _dot(qk_s[rows, :], v_new)
    egl = egl_s[pl.ds(pl.multiple_of(c * SUBLANES, SUBLANES), 1), :]
    return s * egl + _dot(kdt_s[rows, :], v_new)


def _dn_scan(n, need_o, state, q_s, k_s, v_s, bfrow_s, bbrow_s, gfrow_s, grrow_s, fwd_s, bwd_s, o_s):
    group = min(n, DN_PREP_GROUP)

    def prep(i, carry):
        _dn_group_prep([i * group + j for j in range(group)], need_o, q_s, k_s, v_s, bfrow_s, bbrow_s, gfrow_s,
                       grrow_s, fwd_s, bwd_s)
        return carry

    lax.fori_loop(0, n // group, prep, 0)

    def step(c, carry):
        s_f, s_b = carry
        s_f = _dn_chunk_step(c, s_f, need_o, *fwd_s, o_s)
        s_b = _dn_chunk_step(n - 1 - c, s_b, need_o, *bwd_s, o_s)
        return s_f, s_b

    return lax.fori_loop(0, n, step, state)


def _deltanet_kernel(qc_ref, kc_ref, vc_ref, gbtc_ref, ql_ref, kl_ref, vl_ref, zl_ref, gbtl_ref,
                     cwq_ref, cwk_ref, cwv_ref, gnw_ref, out_ref,
                     q_s, k_s, v_s, bfrow_s, bbrow_s, gfrow_s, grrow_s,
                     uf_s, wf_s, qkf_s, qdf_s, kdtf_s, eglf_s, ub_s, wb_s, qkb_s, qdb_s, kdtb_s, eglb_s, o_s):
    h = pl.program_id(1)
    t_ctx, t_lat = qc_ref.shape[0], ql_ref.shape[0]
    vec_s = (q_s, k_s, v_s, bfrow_s, bbrow_s, gfrow_s, grrow_s)
    fwd_s = (uf_s, wf_s, qkf_s, qdf_s, kdtf_s, eglf_s)
    bwd_s = (ub_s, wb_s, qkb_s, qdb_s, kdtb_s, eglb_s)
    zero = jnp.zeros((DN_HEAD_DIM, DN_HEAD_DIM), F32)

    _dn_load(t_ctx, h, qc_ref, kc_ref, vc_ref, gbtc_ref, cwq_ref, cwk_ref, cwv_ref, *vec_s)
    state = _dn_scan(t_ctx // DN_BLOCK, False, (zero, zero), *vec_s, fwd_s, bwd_s, o_s)

    _dn_load(t_lat, h, ql_ref, kl_ref, vl_ref, gbtl_ref, cwq_ref, cwk_ref, cwv_ref, *vec_s)
    o_s[...] = jnp.zeros_like(o_s)
    _dn_scan(t_lat // DN_BLOCK, True, state, *vec_s, fwd_s, bwd_s, o_s)

    o = o_s[...]
    z = zl_ref[...].astype(F32)
    o = o * lax.rsqrt(jnp.mean(o * o, axis=-1, keepdims=True) + NORM_EPS) * gnw_ref[...]
    out_ref[...] = (o * (z * _sigmoid(z))).astype(out_ref.dtype)


def _deltanet(pc_qkv, gbt_c, p_main, gbt_l, conv_w, gn_w, bsz):
    t_ctx = pc_qkv.shape[0] // bsz
    t_lat = p_main.shape[0] // bsz
    hd, nh = DN_HEAD_DIM, DN_HEADS
    col = lambda off: (lambda b, h: (b, off + h))
    col0 = lambda b, h: (0, b)
    wcol = lambda off: (lambda b, h: (0, off + h))
    vec = lambda dt: pltpu.VMEM((t_lat, hd), dt)
    rowvec = pltpu.VMEM((SUBLANES, t_lat), F32)
    per_dir = [vec(F32), vec(BF16), vec(BF16), vec(BF16), vec(BF16),
               pltpu.VMEM((t_lat // DN_BLOCK * SUBLANES, LANES), F32)]
    return pl.pallas_call(
        _deltanet_kernel,
        grid=(bsz, nh),
        in_specs=[pl.BlockSpec((t_ctx, hd), col(0)), pl.BlockSpec((t_ctx, hd), col(nh)),
                  pl.BlockSpec((t_ctx, hd), col(2 * nh)), pl.BlockSpec((LANES, t_ctx), col0),
                  pl.BlockSpec((t_lat, hd), col(0)), pl.BlockSpec((t_lat, hd), col(nh)),
                  pl.BlockSpec((t_lat, hd), col(2 * nh)), pl.BlockSpec((t_lat, hd), col(3 * nh)),
                  pl.BlockSpec((LANES, t_lat), col0),
                  pl.BlockSpec((3, hd), wcol(0)), pl.BlockSpec((3, hd), wcol(nh)),
                  pl.BlockSpec((3, hd), wcol(2 * nh)),
                  pl.BlockSpec((1, hd), lambda b, h: (0, 0))],
        out_specs=pl.BlockSpec((t_lat, hd), lambda b, h: (b, h)),
        out_shape=jax.ShapeDtypeStruct((p_main.shape[0], DN_WIDTH), BF16),
        scratch_shapes=[vec(F32)] * 3 + [rowvec] * 4 + per_dir + per_dir + [vec(F32)],
        compiler_params=_cparams(("parallel", "arbitrary")),
        name="deltanet",
    )(pc_qkv, pc_qkv, pc_qkv, gbt_c, p_main, p_main, p_main, p_main, gbt_l,
      conv_w, conv_w, conv_w, gn_w.reshape(1, hd))


def _mix_kernel(a_ref, bg_ref, cg_ref, xin_ref, cw_ref, ga_ref, gb_ref, wdn_ref, wsc_ref, o_ref, b_scr):
    @pl.when(pl.program_id(1) == 0)
    def _():
        zc = cg_ref[...].astype(F32) * xin_ref[...].astype(F32)
        tm = zc.shape[0]
        pos = lax.broadcasted_iota(jnp.int32, zc.shape, 0) % GRID_W
        prev = jnp.where(pos == 0, 0.0, pltpu.roll(zc, 1, 0))
        nxt = jnp.where(pos == GRID_W - 1, 0.0, pltpu.roll(zc, tm - 1, 0))
        cw = cw_ref[...]
        conv = cw[0:1, :] * prev + cw[1:2, :] * zc + cw[2:3, :] * nxt
        b_scr[...] = (bg_ref[...].astype(F32) * conv).astype(BF16)

    ya = _dot(a_ref[...], wdn_ref[...])
    yb = _dot(b_scr[...], wsc_ref[...])
    mix = _sigmoid(ga_ref[...].astype(F32)) * ya + _sigmoid(gb_ref[...].astype(F32)) * yb
    o_ref[...] = mix.astype(o_ref.dtype)


def _mix(a_in, p_main, conv_sc, w_dn, w_sc, tm, tj):
    m = a_in.shape[0]
    d = w_dn.shape[1]
    ga0, gb0 = COL_GA // tj, COL_GB // tj
    sc0 = COL_SC // SC_WIDTH
    const = lambda i, j: (0, 0)
    rowblk = lambda i, j: (i, 0)
    return pl.pallas_call(
        _mix_kernel,
        grid=(m // tm, d // tj),
        in_specs=[pl.BlockSpec((tm, DN_WIDTH), rowblk),
                  pl.BlockSpec((tm, SC_WIDTH), lambda i, j: (i, sc0)),
                  pl.BlockSpec((tm, SC_WIDTH), lambda i, j: (i, sc0 + 1)),
                  pl.BlockSpec((tm, SC_WIDTH), lambda i, j: (i, sc0 + 2)),
                  pl.BlockSpec((3, SC_WIDTH), const),
                  pl.BlockSpec((tm, tj), lambda i, j: (i, ga0 + j)),
                  pl.BlockSpec((tm, tj), lambda i, j: (i, gb0 + j)),
                  pl.BlockSpec((DN_WIDTH, tj), lambda i, j: (0, j)),
                  pl.BlockSpec((SC_WIDTH, tj), lambda i, j: (0, j))],
        out_specs=pl.BlockSpec((tm, tj), lambda i, j: (i, j)),
        out_shape=jax.ShapeDtypeStruct((m, d), BF16),
        scratch_shapes=[pltpu.VMEM((tm, SC_WIDTH), BF16)],
        compiler_params=_cparams(("parallel", "arbitrary")),
        name="branch_mix",
    )(a_in, p_main, p_main, p_main, conv_sc, p_main, p_main, w_dn, w_sc)


def _oproj_kernel(mix_ref, wo_ref, x_ref, gt_ref, sh_ref, sc_ref, nw_ref, wr_ref, x1_ref, h2_ref, lg_ref):
    x1 = x_ref[...] + gt_ref[0] * _dot(mix_ref[...], wo_ref[...])
    x1_ref[...] = x1
    y = x1 * lax.rsqrt(jnp.mean(x1 * x1, axis=-1, keepdims=True) + NORM_EPS) * nw_ref[...]
    hb = (y * (1.0 + sc_ref[0]) + sh_ref[0]).astype(BF16)
    h2_ref[...] = hb
    lg_ref[...] = _dot(hb, wr_ref[...])


def _oproj(mix, w_o, x2d, gate, shift, scale, norm_w, w_r, rows_per_mod, tm):
    m, d = x2d.shape
    nb = gate.shape[0]
    mod_idx = lambda i: ((i * tm) // rows_per_mod, 0, 0)
    const = lambda i: (0, 0)
    rowblk = lambda i: (i, 0)
    resident = pl.Buffered(1)
    return pl.pallas_call(
        _oproj_kernel,
        grid=(m // tm,),
        in_specs=[pl.BlockSpec((tm, d), rowblk),
                  pl.BlockSpec((d, d), const, pipeline_mode=resident),
                  pl.BlockSpec((tm, d), rowblk),
                  pl.BlockSpec((1, 1, d), mod_idx),
                  pl.BlockSpec((1, 1, d), mod_idx),
                  pl.BlockSpec((1, 1, d), mod_idx),
                  pl.BlockSpec((1, d), const),
                  pl.BlockSpec((d, LANES), const, pipeline_mode=resident)],
        out_specs=[pl.BlockSpec((tm, d), rowblk),
                   pl.BlockSpec((tm, d), rowblk),
                   pl.BlockSpec((tm, LANES), rowblk)],
        out_shape=[jax.ShapeDtypeStruct((m, d), F32),
                   jax.ShapeDtypeStruct((m, d), BF16),
                   jax.ShapeDtypeStruct((m, LANES), F32)],
        compiler_params=_cparams(("parallel",)),
        name="out_proj",
    )(mix, w_o, x2d, gate.reshape(nb, 1, d), shift.reshape(nb, 1, d), scale.reshape(nb, 1, d),
      norm_w.reshape(1, d), w_r)


def _moe_kernel(xs_ref, wg_ref, wu_ref, wd_ref, aff_ref, o_ref, acc):
    f = pl.program_id(2)
    x = xs_ref[0]
    g = _dot(x, wg_ref[0].astype(BF16))
    u = _dot(x, wu_ref[0].astype(BF16))
    hid = (g * _sigmoid(g) * u).astype(BF16)
    contrib = _dot(hid, wd_ref[0].astype(BF16))

    @pl.when(f == 0)
    def _():
        acc[...] = contrib

    @pl.when(f > 0)
    def _():
        acc[...] += contrib

    @pl.when(f == pl.num_programs(2) - 1)
    def _():
        o_ref[0] = (acc[...] * aff_ref[0]).astype(o_ref.dtype)


def _moe_ffn(xs, w_gate, w_up, w_down, aff, tm, tf):
    e, m, d = xs.shape
    ff = w_gate.shape[2]
    return pl.pallas_call(
        _moe_kernel,
        grid=(e, m // tm, ff // tf),
        in_specs=[pl.BlockSpec((1, tm, d), lambda e_, i, f: (e_, i, 0)),
                  pl.BlockSpec((1, d, tf), lambda e_, i, f: (e_, 0, f)),
                  pl.BlockSpec((1, d, tf), lambda e_, i, f: (e_, 0, f)),
                  pl.BlockSpec((1, tf, d), lambda e_, i, f: (e_, f, 0)),
                  pl.BlockSpec((1, tm, 1), lambda e_, i, f: (e_, i, 0))],
        out_specs=pl.BlockSpec((1, tm, d), lambda e_, i, f: (e_, i, 0)),
        out_shape=jax.ShapeDtypeStruct((e, m, d), BF16),
        scratch_shapes=[pltpu.VMEM((tm, d), F32)],
        compiler_params=_cparams(("parallel", "parallel", "arbitrary")),
        name="moe_ffn",
    )(xs, w_gate, w_up, w_down, aff)


def _combine_kernel(idx_ref, ye_ref, x_ref, gt_ref, nw_ref, o_ref, acc):
    g = pl.program_id(2)
    tt, d = acc.shape
    idx = idx_ref[0, 0]
    tok = lax.broadcasted_iota(jnp.int32, (tt, idx.shape[-1]), 0) + pl.program_id(1) * tt
    onehot = jnp.where(tok == idx, 1.0, 0.0).astype(BF16)
    contrib = _dot(onehot, ye_ref[...].reshape(idx.shape[-1], d))

    @pl.when(g == 0)
    def _():
        acc[...] = contrib

    @pl.when(g > 0)
    def _():
        acc[...] += contrib

    @pl.when(g == pl.num_programs(2) - 1)
    def _():
        x2 = x_ref[...] + gt_ref[0] * acc[...]
        o_ref[...] = x2 * lax.rsqrt(jnp.mean(x2 * x2, axis=-1, keepdims=True) + NORM_EPS) * nw_ref[...]


def _combine(idx, ye, x1, gate, norm_w, seq, tt, eg):
    bsz, ne, cap = idx.shape
    m, d = x1.shape
    tiles = seq // tt
    return pl.pallas_call(
        _combine_kernel,
        grid=(bsz, tiles, ne // eg),
        in_specs=[pl.BlockSpec((1, 1, 1, eg * cap), lambda b, t, g: (b, g, 0, 0)),
                  pl.BlockSpec((eg, cap, d), lambda b, t, g: (g, b, 0)),
                  pl.BlockSpec((tt, d), lambda b, t, g: (b * tiles + t, 0)),
                  pl.BlockSpec((1, 1, d), lambda b, t, g: (b, 0, 0)),
                  pl.BlockSpec((1, d), lambda b, t, g: (0, 0))],
        out_specs=pl.BlockSpec((tt, d), lambda b, t, g: (b * tiles + t, 0)),
        out_shape=jax.ShapeDtypeStruct((m, d), F32),
        scratch_shapes=[pltpu.VMEM((tt, d), F32)],
        compiler_params=_cparams(("parallel", "parallel", "arbitrary")),
        name="combine_final",
    )(idx.reshape(bsz, ne // eg, 1, eg * cap), ye, x1, gate.reshape(bsz, 1, d), norm_w.reshape(1, d))


def kernel(x, c, ctx, c_ctx, w_ada, b_ada, norm1, norm2, w_in, conv_qkv, a_log, dt_bias, gn_w, w_dn_out,
           conv_sc, w_sc_out, w_o, w_router, w_gate, w_up, w_down, norm_f):
    bsz, seq, d = x.shape
    m = bsz * seq
    l = 0

    pad = (-(bsz + 1)) % SUBLANES
    c_all = jnp.concatenate([c, c_ctx[None, :], jnp.zeros((pad, d), F32)], axis=0)
    mod_all = _ada(c_all, w_ada[l], b_ada[l])
    sh1, sc1, gt1, sh2, sc2, gt2 = jnp.split(mod_all[:bsz], 6, axis=-1)
    csh1, csc1 = mod_all[bsz:bsz + 1, :d], mod_all[bsz:bsz + 1, d:2 * d]

    wi = w_in[l]
    w_main = jnp.concatenate([wi[:, :N_QKV], wi[:, N_STATE_COLS:]], axis=1).astype(BF16)
    w_ba_t = jnp.pad(wi[:, N_QKV:N_STATE_COLS], ((0, 0), (0, LANES - N_BA))).astype(BF16).T
    lane_par = lambda a: jnp.pad(a.reshape(-1), (2 * DN_HEADS, LANES - N_BA))
    alog, dtb = lane_par(a_log[l]), lane_par(dt_bias[l])

    x2d = x.reshape(m, d)
    p_main, gbt_l = _in_proj(x2d, sh1, sc1, norm1[l], w_main, w_ba_t, alog, dtb, N_MAIN, seq, 1024, 1024)
    pc_qkv, gbt_c = _in_proj(ctx.reshape(bsz * CTX_LEN, d), csh1, csc1, norm1[l], w_main, w_ba_t, alog, dtb,
                             N_QKV, bsz * CTX_LEN, 1024, 1024)

    a_in = _deltanet(pc_qkv, gbt_c, p_main, gbt_l, conv_qkv[l], gn_w[l], bsz)

    w_r = jnp.pad(w_router[l], ((0, 0), (0, LANES - N_EXPERTS))).astype(BF16)
    mix = _mix(a_in, p_main, conv_sc[l], w_dn_out[l].astype(BF16), w_sc_out[l].astype(BF16), 1024, 512)
    x1, h2, logits = _oproj(mix, w_o[l].astype(BF16), x2d, gt1, sh2, sc2, norm2[l], w_r, seq, 512)

    cap = CAPACITY_FACTOR * seq // N_EXPERTS
    aff = jax.nn.softmax(logits[:, :N_EXPERTS].reshape(bsz, seq, N_EXPERTS), axis=-1)
    top_aff, idx = lax.top_k(aff.transpose(0, 2, 1), cap)
    flat_idx = (idx + (jnp.arange(bsz, dtype=idx.dtype) * seq)[:, None, None]).transpose(1, 0, 2)
    flat_idx = flat_idx.reshape(N_EXPERTS, bsz * cap)
    xs = jnp.take(h2, flat_idx, axis=0)
    aff_e = top_aff.transpose(1, 0, 2).reshape(N_EXPERTS, bsz * cap, 1)
    ye = _moe_ffn(xs, w_gate[l], w_up[l], w_down[l], aff_e, 1024, 256)
    out = _combine(idx, ye, x1, gt2, norm_f, seq, 512, 8)
    return out.reshape(bsz, seq, d)
```

```python
import functools

import jax
import jax.numpy as jnp
from jax import lax
from jax.experimental import pallas as pl
from jax.experimental.pallas import tpu as pltpu

F32 = jnp.float32
BF16 = jnp.bfloat16

D_MODEL = 2048
CTX_LEN = 256
GRID_W = 64
DN_HEADS = 8
DN_HEAD_DIM = 128
DN_WIDTH = DN_HEADS * DN_HEAD_DIM
SC_WIDTH = 1024
N_EXPERTS = 16
EXPERT_FF = 2048
CAPACITY_FACTOR = 2
NORM_EPS = 1e-6
N_QKV = 3 * DN_WIDTH
N_BA = 4 * DN_HEADS
N_STATE_COLS = N_QKV + N_BA
LANES = 128
SUBLANES = 8
COL_Z = N_QKV
COL_SC = COL_Z + DN_WIDTH
COL_GA = COL_SC + 3 * SC_WIDTH
COL_GB = COL_GA + D_MODEL
N_MAIN = COL_GB + D_MODEL
DN_BLOCK = LANES
DN_INVERSE_LEVELS = 7
DN_PREP_GROUP = 4
DN_HEADS_PER_STEP = 2

VMEM_LIMIT = 56 * 1024 * 1024


def _cparams(sem):
    return pltpu.CompilerParams(dimension_semantics=sem, vmem_limit_bytes=VMEM_LIMIT)


def _dot(a, b):
    return jnp.dot(a, b, preferred_element_type=F32)


def _dot_nt(a, b):
    return lax.dot_general(a, b, (((1,), (1,)), ((), ())), preferred_element_type=F32)


def _sigmoid(x):
    return 0.5 * jnp.tanh(0.5 * x) + 0.5


def _softplus(x):
    return jnp.maximum(x, 0.0) + jnp.log(1.0 + jnp.exp(-jnp.abs(x)))


def _ada_kernel(c_ref, w_ref, b_ref, o_ref):
    c = c_ref[...]
    s = (c * _sigmoid(c)).astype(BF16)
    o_ref[...] = _dot(s, w_ref[...].astype(BF16)) + b_ref[...]


def _ada(c_all, w, b):
    rows, d = c_all.shape
    n = w.shape[1]
    tn = 1024
    return pl.pallas_call(
        _ada_kernel,
        grid=(n // tn,),
        in_specs=[pl.BlockSpec((rows, d), lambda j: (0, 0)),
                  pl.BlockSpec((d, tn), lambda j: (0, j)),
                  pl.BlockSpec((1, tn), lambda j: (0, j))],
        out_specs=pl.BlockSpec((rows, tn), lambda j: (0, j)),
        out_shape=jax.ShapeDtypeStruct((rows, n), F32),
        compiler_params=_cparams(("arbitrary",)),
        name="ada_ln",
    )(c_all, w, b.reshape(1, n))


def _gate_act(raw, idx, alog, dtb):
    return jnp.where(idx < 2 * DN_HEADS, _sigmoid(raw), -jnp.exp(alog) * _softplus(raw + dtb))


def _in_proj_kernel(x_ref, sh_ref, sc_ref, nw_ref, w_ref, wbat_ref, alogt_ref, dtbt_ref, o_ref, gbt_ref, h_scr):
    @pl.when(pl.program_id(1) == 0)
    def _():
        x = x_ref[...]
        y = x * lax.rsqrt(jnp.mean(x * x, axis=-1, keepdims=True) + NORM_EPS) * nw_ref[...]
        hb = (y * (1.0 + sc_ref[0]) + sh_ref[0]).astype(BF16)
        h_scr[...] = hb
        raw_t = _dot_nt(wbat_ref[...], hb)
        gbt_ref[...] = _gate_act(raw_t, lax.broadcasted_iota(jnp.int32, raw_t.shape, 0), alogt_ref[...],
                                 dtbt_ref[...])

    o_ref[...] = _dot(h_scr[...], w_ref[...]).astype(o_ref.dtype)


def _in_proj(x2d, shift, scale, norm_w, w_main, w_ba_t, alog, dtb, n_out, rows_per_mod, tm, tn):
    m, d = x2d.shape
    nb = shift.shape[0]
    mod_idx = lambda i, j: ((i * tm) // rows_per_mod, 0, 0)
    const = lambda i, j: (0, 0)
    return pl.pallas_call(
        _in_proj_kernel,
        grid=(m // tm, n_out // tn),
        in_specs=[pl.BlockSpec((tm, d), lambda i, j: (i, 0)),
                  pl.BlockSpec((1, 1, d), mod_idx),
                  pl.BlockSpec((1, 1, d), mod_idx),
                  pl.BlockSpec((1, d), const),
                  pl.BlockSpec((d, tn), lambda i, j: (0, j)),
                  pl.BlockSpec((LANES, d), const),
                  pl.BlockSpec((LANES, 1), const),
                  pl.BlockSpec((LANES, 1), const)],
        out_specs=[pl.BlockSpec((tm, tn), lambda i, j: (i, j)),
                   pl.BlockSpec((LANES, tm), lambda i, j: (0, i))],
        out_shape=[jax.ShapeDtypeStruct((m, n_out), BF16),
                   jax.ShapeDtypeStruct((LANES, m), F32)],
        scratch_shapes=[pltpu.VMEM((tm, d), BF16)],
        compiler_params=_cparams(("parallel", "arbitrary")),
        name="in_proj",
    )(x2d, shift.reshape(nb, 1, d), scale.reshape(nb, 1, d), norm_w.reshape(1, d), w_main, w_ba_t,
      alog.reshape(LANES, 1), dtb.reshape(LANES, 1))


def _conv_silu(x, w, t):
    row = lax.broadcasted_iota(jnp.int32, x.shape, 0)
    prev = jnp.where(row == 0, 0.0, pltpu.roll(x, 1, 0))
    nxt = jnp.where(row == t - 1, 0.0, pltpu.roll(x, t - 1, 0))
    y = w[0:1, :] * prev + w[1:2, :] * x + w[2:3, :] * nxt
    return y * _sigmoid(y)


def _chunk_cumsum(x, axis, reverse):
    n = x.shape[axis]
    pos = lax.broadcasted_iota(jnp.int32, x.shape, axis) % DN_BLOCK
    s = 1
    while s < DN_BLOCK:
        if reverse:
            x = x + jnp.where(pos + s < DN_BLOCK, pltpu.roll(x, n - s, axis), 0.0)
        else:
            x = x + jnp.where(pos >= s, pltpu.roll(x, s, axis), 0.0)
        s *= 2
    return x


def _dn_load(t, h, q_ref, k_ref, v_ref, gbt_ref, cwq_ref, cwk_ref, cwv_ref, q_s, k_s, v_s, bfrow_s, bbrow_s, gfrow_s,
             grrow_s):
    q = _conv_silu(q_ref[...].astype(F32), cwq_ref[...], t)
    q_s[0:t, :] = q * lax.rsqrt(jnp.sum(q * q, axis=-1, keepdims=True) + NORM_EPS) * (DN_HEAD_DIM ** -0.5)
    k = _conv_silu(k_ref[...].astype(F32), cwk_ref[...], t)
    k_s[0:t, :] = k * lax.rsqrt(jnp.sum(k * k, axis=-1, keepdims=True) + NORM_EPS)
    v_s[0:t, :] = _conv_silu(v_ref[...].astype(F32), cwv_ref[...], t)
    for dst, base, reverse in ((bfrow_s, 0, None), (bbrow_s, DN_HEADS, None), (gfrow_s, 2 * DN_HEADS, False),
                               (grrow_s, 3 * DN_HEADS, True)):
        grp = gbt_ref[base:base + DN_HEADS, :]
        sub = lax.broadcasted_iota(jnp.int32, grp.shape, 0)
        val = jnp.broadcast_to(jnp.sum(jnp.where(sub == h, grp, 0.0), axis=0, keepdims=True), grp.shape)
        dst[:, 0:t] = val if reverse is None else _chunk_cumsum(val, 1, reverse)


def _dn_group_prep(chunks, need_o, q_s, k_s, v_s, bfrow_s, bbrow_s, gfrow_s, grrow_s, fwd_s, bwd_s):
    L = DN_BLOCK
    ri = lax.broadcasted_iota(jnp.int32, (L, L), 0)
    ci = lax.broadcasted_iota(jnp.int32, (L, L), 1)
    blk = ri ^ ci
    eye = jnp.where(ri == ci, 1.0, 0.0)
    chains = []
    for c in chunks:
        r0 = pl.multiple_of(c * L, L)
        rows = pl.ds(r0, L)
        q, k, v = q_s[rows, :], k_s[rows, :], v_s[rows, :]
        k16 = k.astype(BF16)
        kk = _dot_nt(k16, k16)
        qk = _dot_nt(q.astype(BF16), k16) if need_o else None
        for reverse, brow_s, grow_s, outs in ((False, bfrow_s, gfrow_s, fwd_s), (True, bbrow_s, grrow_s, bwd_s)):
            gc_row = jnp.broadcast_to(grow_s[0:1, pl.ds(r0, L)], (L, L))
            gc_col = jnp.transpose(gc_row)
            beta = jnp.transpose(jnp.broadcast_to(brow_s[0:1, pl.ds(r0, L)], (L, L)))
            incl, strict = (ri <= ci, ri < ci) if reverse else (ri >= ci, ri > ci)
            decay = jnp.exp(jnp.where(incl, gc_col - gc_row, -jnp.inf))
            a_tri = jnp.where(strict, beta * kk * decay, 0.0)
            g_last = gc_col[0:1, :] if reverse else gc_col[L - 1:L, :]
            chains.append(dict(c=c, rows=rows, q=q, k=k, v=v, qk=qk, beta=beta, gc_col=gc_col, decay=decay,
                               a_tri=a_tri, g_last=g_last, t_inv=eye, outs=outs))
    for level in range(DN_INVERSE_LEVELS):
        for ch in chains:
            ch["t16"] = ch["t_inv"].astype(BF16)
            a_off = jnp.where((blk >> level) == 1, ch["a_tri"], 0.0).astype(BF16)
            ch["m"] = _dot(a_off, ch["t16"]).astype(BF16)
        for ch in chains:
            ch["t_inv"] = ch["t_inv"] - _dot(ch["t16"], ch["m"])
    for ch in chains:
        egc = jnp.exp(ch["gc_col"])
        rhs = jnp.concatenate([ch["v"] * ch["beta"], ch["k"] * ch["beta"] * egc], axis=-1).astype(BF16)
        ch["sol"] = _dot(ch["t_inv"].astype(BF16), rhs)
        ch["egc"] = egc
    for ch in chains:
        u_s, w_s, qk_s, qd_s, kdt_s, egl_s = ch["outs"]
        rows, c = ch["rows"], ch["c"]
        u_s[rows, :] = ch["sol"][:, :DN_HEAD_DIM]
        w_s[rows, :] = ch["sol"][:, DN_HEAD_DIM:].astype(BF16)
        kd = ch["k"] * jnp.exp(ch["g_last"] - ch["gc_col"])
        kdt_s[rows, :] = jnp.transpose(kd).astype(BF16)
        egl_s[pl.ds(pl.multiple_of(c * SUBLANES, SUBLANES), SUBLANES), :] = jnp.broadcast_to(
            jnp.exp(ch["g_last"]), (SUBLANES, LANES))
        if need_o:
            qk_s[rows, :] = (ch["qk"] * ch["decay"]).astype(BF16)
            qd_s[rows, :] = (ch["q"] * ch["egc"]).astype(BF16)


def _dn_scan(n, need_o, states, heads):
    group = min(n, DN_PREP_GROUP)
    for vec_s, fwd_s, bwd_s, _ in heads:
        def prep(i, carry, vec_s=vec_s, fwd_s=fwd_s, bwd_s=bwd_s):
            _dn_group_prep([i * group + j for j in range(group)], need_o, *vec_s, fwd_s, bwd_s)
            return carry

        lax.fori_loop(0, n // group, prep, 0)

    def step(c, carry):
        chains = []
        for (s_f, s_b), (_, fwd_s, bwd_s, o_s) in zip(carry, heads):
            chains.append(dict(s=s_f, c=c, sc=fwd_s, o_s=o_s))
            chains.append(dict(s=s_b, c=n - 1 - c, sc=bwd_s, o_s=o_s))
        for ch in chains:
            u_s, w_s, qk_s, qd_s, kdt_s, egl_s = ch["sc"]
            ch["rows"] = pl.ds(pl.multiple_of(ch["c"] * DN_BLOCK, DN_BLOCK), DN_BLOCK)
            s16 = ch["s"].astype(BF16)
            ch["ws"] = _dot(w_s[ch["rows"], :], s16)
            if need_o:
                ch["qs"] = _dot(qd_s[ch["rows"], :], s16)
        for ch in chains:
            u_s, w_s, qk_s, qd_s, kdt_s, egl_s = ch["sc"]
            v_new = (u_s[ch["rows"], :] - ch["ws"]).astype(BF16)
            ch["kv"] = _dot(kdt_s[ch["rows"], :], v_new)
            if need_o:
                ch["qv"] = _dot(qk_s[ch["rows"], :], v_new)
        out = []
        for ch in chains:
            egl_s = ch["sc"][5]
            egl = egl_s[pl.ds(pl.multiple_of(ch["c"] * SUBLANES, SUBLANES), 1), :]
            out.append(ch["s"] * egl + ch["kv"])
            if need_o:
                ch["o_s"][ch["rows"], :] += ch["qs"] + ch["qv"]
        return tuple((out[2 * i], out[2 * i + 1]) for i in range(len(heads)))

    return lax.fori_loop(0, n, step, tuple(states))


def _deltanet_kernel(qc_ref, kc_ref, vc_ref, gbtc_ref, ql_ref, kl_ref, vl_ref, zl_ref, gbtl_ref,
                     cwq_ref, cwk_ref, cwv_ref, gnw_ref, out_ref, *scratch):
    hd = DN_HEAD_DIM
    n_heads = out_ref.shape[1] // hd
    per_head = len(scratch) // n_heads
    t_ctx, t_lat = qc_ref.shape[0], ql_ref.shape[0]
    heads = []
    for i in range(n_heads):
        sc = scratch[i * per_head:(i + 1) * per_head]
        heads.append((sc[0:7], sc[7:13], sc[13:19], sc[19]))
    lanes = lambda ref, i: ref.at[:, pl.ds(i * hd, hd)]
    zero = jnp.zeros((hd, hd), F32)

    for i, (vec_s, _, _, _) in enumerate(heads):
        _dn_load(t_ctx, pl.program_id(1) * n_heads + i, lanes(qc_ref, i), lanes(kc_ref, i), lanes(vc_ref, i), gbtc_ref,
                 lanes(cwq_ref, i), lanes(cwk_ref, i), lanes(cwv_ref, i), *vec_s)
    states = _dn_scan(t_ctx // DN_BLOCK, False, [(zero, zero)] * n_heads, heads)

    for i, (vec_s, _, _, o_s) in enumerate(heads):
        _dn_load(t_lat, pl.program_id(1) * n_heads + i, lanes(ql_ref, i), lanes(kl_ref, i), lanes(vl_ref, i), gbtl_ref,
                 lanes(cwq_ref, i), lanes(cwk_ref, i), lanes(cwv_ref, i), *vec_s)
        o_s[...] = jnp.zeros_like(o_s)
    _dn_scan(t_lat // DN_BLOCK, True, states, heads)

    for i, (_, _, _, o_s) in enumerate(heads):
        o = o_s[...]
        z = zl_ref[:, i * hd:(i + 1) * hd].astype(F32)
        o = o * lax.rsqrt(jnp.mean(o * o, axis=-1, keepdims=True) + NORM_EPS) * gnw_ref[...]
        out_ref[:, i * hd:(i + 1) * hd] = (o * (z * _sigmoid(z))).astype(out_ref.dtype)


def _deltanet(pc_qkv, gbt_c, p_main, gbt_l, conv_w, gn_w, bsz):
    t_ctx = pc_qkv.shape[0] // bsz
    t_lat = p_main.shape[0] // bsz
    hd, nh, hps = DN_HEAD_DIM, DN_HEADS, DN_HEADS_PER_STEP
    wide = hps * hd
    col = lambda off: (lambda b, h: (b, off // hps + h))
    col0 = lambda b, h: (0, b)
    wcol = lambda off: (lambda b, h: (0, off // hps + h))
    vec = lambda dt: pltpu.VMEM((t_lat, hd), dt)
    rowvec = pltpu.VMEM((SUBLANES, t_lat), F32)
    per_dir = [vec(F32), vec(BF16), vec(BF16), vec(BF16), vec(BF16),
               pltpu.VMEM((t_lat // DN_BLOCK * SUBLANES, LANES), F32)]
    per_head = [vec(F32)] * 3 + [rowvec] * 4 + per_dir + per_dir + [vec(F32)]
    return pl.pallas_call(
        _deltanet_kernel,
        grid=(bsz, nh // hps),
        in_specs=[pl.BlockSpec((t_ctx, wide), col(0)), pl.BlockSpec((t_ctx, wide), col(nh)),
                  pl.BlockSpec((t_ctx, wide), col(2 * nh)), pl.BlockSpec((LANES, t_ctx), col0),
                  pl.BlockSpec((t_lat, wide), col(0)), pl.BlockSpec((t_lat, wide), col(nh)),
                  pl.BlockSpec((t_lat, wide), col(2 * nh)), pl.BlockSpec((t_lat, wide), col(3 * nh)),
                  pl.BlockSpec((LANES, t_lat), col0),
                  pl.BlockSpec((3, wide), wcol(0)), pl.BlockSpec((3, wide), wcol(nh)),
                  pl.BlockSpec((3, wide), wcol(2 * nh)),
                  pl.BlockSpec((1, hd), lambda b, h: (0, 0))],
        out_specs=pl.BlockSpec((t_lat, wide), lambda b, h: (b, h)),
        out_shape=jax.ShapeDtypeStruct((p_main.shape[0], DN_WIDTH), BF16),
        scratch_shapes=per_head * hps,
        compiler_params=_cparams(("parallel", "arbitrary")),
        name="deltanet",
    )(pc_qkv, pc_qkv, pc_qkv, gbt_c, p_main, p_main, p_main, p_main, gbt_l,
      conv_w, conv_w, conv_w, gn_w.reshape(1, hd))


def _mix_kernel(a_ref, bg_ref, cg_ref, xin_ref, cw_ref, ga_ref, gb_ref, wdn_ref, wsc_ref, o_ref, b_scr):
    @pl.when(pl.program_id(1) == 0)
    def _():
        zc = cg_ref[...].astype(F32) * xin_ref[...].astype(F32)
        tm = zc.shape[0]
        pos = lax.broadcasted_iota(jnp.int32, zc.shape, 0) % GRID_W
        prev = jnp.where(pos == 0, 0.0, pltpu.roll(zc, 1, 0))
        nxt = jnp.where(pos == GRID_W - 1, 0.0, pltpu.roll(zc, tm - 1, 0))
        cw = cw_ref[...]
        conv = cw[0:1, :] * prev + cw[1:2, :] * zc + cw[2:3, :] * nxt
        b_scr[...] = (bg_ref[...].astype(F32) * conv).astype(BF16)

    ya = _dot(a_ref[...], wdn_ref[...])
    yb = _dot(b_scr[...], wsc_ref[...])
    mix = _sigmoid(ga_ref[...].astype(F32)) * ya + _sigmoid(gb_ref[...].astype(F32)) * yb
    o_ref[...] = mix.astype(o_ref.dtype)


def _mix(a_in, p_main, conv_sc, w_dn, w_sc, tm, tj):
    m = a_in.shape[0]
    d = w_dn.shape[1]
    ga0, gb0 = COL_GA // tj, COL_GB // tj
    sc0 = COL_SC // SC_WIDTH
    const = lambda i, j: (0, 0)
    rowblk = lambda i, j: (i, 0)
    return pl.pallas_call(
        _mix_kernel,
        grid=(m // tm, d // tj),
        in_specs=[pl.BlockSpec((tm, DN_WIDTH), rowblk),
                  pl.BlockSpec((tm, SC_WIDTH), lambda i, j: (i, sc0)),
                  pl.BlockSpec((tm, SC_WIDTH), lambda i, j: (i, sc0 + 1)),
                  pl.BlockSpec((tm, SC_WIDTH), lambda i, j: (i, sc0 + 2)),
                  pl.BlockSpec((3, SC_WIDTH), const),
                  pl.BlockSpec((tm, tj), lambda i, j: (i, ga0 + j)),
                  pl.BlockSpec((tm, tj), lambda i, j: (i, gb0 + j)),
                  pl.BlockSpec((DN_WIDTH, tj), lambda i, j: (0, j)),
                  pl.BlockSpec((SC_WIDTH, tj), lambda i, j: (0, j))],
        out_specs=pl.BlockSpec((tm, tj), lambda i, j: (i, j)),
        out_shape=jax.ShapeDtypeStruct((m, d), BF16),
        scratch_shapes=[pltpu.VMEM((tm, SC_WIDTH), BF16)],
        compiler_params=_cparams(("parallel", "arbitrary")),
        name="branch_mix",
    )(a_in, p_main, p_main, p_main, conv_sc, p_main, p_main, w_dn, w_sc)


def _oproj_kernel(mix_ref, wo_ref, x_ref, gt_ref, sh_ref, sc_ref, nw_ref, wr_ref, x1_ref, h2_ref, lg_ref):
    x1 = x_ref[...] + gt_ref[0] * _dot(mix_ref[...], wo_ref[...])
    x1_ref[...] = x1
    y = x1 * lax.rsqrt(jnp.mean(x1 * x1, axis=-1, keepdims=True) + NORM_EPS) * nw_ref[...]
    hb = (y * (1.0 + sc_ref[0]) + sh_ref[0]).astype(BF16)
    h2_ref[...] = hb
    lg_ref[...] = _dot(hb, wr_ref[...])


def _oproj(mix, w_o, x2d, gate, shift, scale, norm_w, w_r, rows_per_mod, tm):
    m, d = x2d.shape
    nb = gate.shape[0]
    mod_idx = lambda i: ((i * tm) // rows_per_mod, 0, 0)
    const = lambda i: (0, 0)
    rowblk = lambda i: (i, 0)
    resident = pl.Buffered(1)
    return pl.pallas_call(
        _oproj_kernel,
        grid=(m // tm,),
        in_specs=[pl.BlockSpec((tm, d), rowblk),
                  pl.BlockSpec((d, d), const, pipeline_mode=resident),
                  pl.BlockSpec((tm, d), rowblk),
                  pl.BlockSpec((1, 1, d), mod_idx),
                  pl.BlockSpec((1, 1, d), mod_idx),
                  pl.BlockSpec((1, 1, d), mod_idx),
                  pl.BlockSpec((1, d), const),
                  pl.BlockSpec((d, LANES), const, pipeline_mode=resident)],
        out_specs=[pl.BlockSpec((tm, d), rowblk),
                   pl.BlockSpec((tm, d), rowblk),
                   pl.BlockSpec((tm, LANES), rowblk)],
        out_shape=[jax.ShapeDtypeStruct((m, d), F32),
                   jax.ShapeDtypeStruct((m, d), BF16),
                   jax.ShapeDtypeStruct((m, LANES), F32)],
        compiler_params=_cparams(("parallel",)),
        name="out_proj",
    )(mix, w_o, x2d, gate.reshape(nb, 1, d), shift.reshape(nb, 1, d), scale.reshape(nb, 1, d),
      norm_w.reshape(1, d), w_r)


def _moe_kernel(xs_ref, wg_ref, wu_ref, wd_ref, aff_ref, o_ref, hid_scr, *, nf, tf):
    s = pl.program_id(2)

    @pl.when(s < nf)
    def _():
        x = xs_ref[0]
        g = _dot(x, wg_ref[0].astype(BF16))
        u = _dot(x, wu_ref[0].astype(BF16))
        col = pl.multiple_of(s * tf, tf)
        hid_scr[:, pl.ds(col, tf)] = (g * _sigmoid(g) * u).astype(BF16)

    @pl.when(s >= nf)
    def _():
        y = _dot(hid_scr[...], wd_ref[0].astype(BF16))
        o_ref[0] = (y * aff_ref[0]).astype(o_ref.dtype)


def _moe_ffn(xs, w_gate, w_up, w_down, aff, tm, tf, tn):
    e, m, d = xs.shape
    ff = w_gate.shape[2]
    nf, nn = ff // tf, d // tn
    up_idx = lambda e_, i, s: (e_, 0, jnp.minimum(s, nf - 1))
    down_idx = lambda e_, i, s: (e_, 0, jnp.maximum(s - nf, 0))
    return pl.pallas_call(
        functools.partial(_moe_kernel, nf=nf, tf=tf),
        grid=(e, m // tm, nf + nn),
        in_specs=[pl.BlockSpec((1, tm, d), lambda e_, i, s: (e_, i, 0)),
                  pl.BlockSpec((1, d, tf), up_idx),
                  pl.BlockSpec((1, d, tf), up_idx),
                  pl.BlockSpec((1, ff, tn), down_idx),
                  pl.BlockSpec((1, tm, 1), lambda e_, i, s: (e_, i, 0))],
        out_specs=pl.BlockSpec((1, tm, tn), lambda e_, i, s: (e_, i, jnp.maximum(s - nf, 0))),
        out_shape=jax.ShapeDtypeStruct((e, m, d), BF16),
        scratch_shapes=[pltpu.VMEM((tm, ff), BF16)],
        compiler_params=_cparams(("parallel", "parallel", "arbitrary")),
        name="moe_ffn",
    )(xs, w_gate, w_up, w_down, aff)


def _combine_kernel(idx_ref, ye_ref, x_ref, gt_ref, nw_ref, o_ref, acc):
    g = pl.program_id(2)
    tt, d = acc.shape
    idx = idx_ref[0, 0]
    tok = lax.broadcasted_iota(jnp.int32, (tt, idx.shape[-1]), 0) + pl.program_id(1) * tt
    onehot = jnp.where(tok == idx, 1.0, 0.0).astype(BF16)
    contrib = _dot(onehot, ye_ref[...].reshape(idx.shape[-1], d))

    @pl.when(g == 0)
    def _():
        acc[...] = contrib

    @pl.when(g > 0)
    def _():
        acc[...] += contrib

    @pl.when(g == pl.num_programs(2) - 1)
    def _():
        x2 = x_ref[...] + gt_ref[0] * acc[...]
        o_ref[...] = x2 * lax.rsqrt(jnp.mean(x2 * x2, axis=-1, keepdims=True) + NORM_EPS) * nw_ref[...]


def _combine(idx, ye, x1, gate, norm_w, seq, tt, eg):
    bsz, ne, cap = idx.shape
    m, d = x1.shape
    tiles = seq // tt
    return pl.pallas_call(
        _combine_kernel,
        grid=(bsz, tiles, ne // eg),
        in_specs=[pl.BlockSpec((1, 1, 1, eg * cap), lambda b, t, g: (b, g, 0, 0)),
                  pl.BlockSpec((eg, cap, d), lambda b, t, g: (g, b, 0)),
                  pl.BlockSpec((tt, d), lambda b, t, g: (b * tiles + t, 0)),
                  pl.BlockSpec((1, 1, d), lambda b, t, g: (b, 0, 0)),
                  pl.BlockSpec((1, d), lambda b, t, g: (0, 0))],
        out_specs=pl.BlockSpec((tt, d), lambda b, t, g: (b * tiles + t, 0)),
        out_shape=jax.ShapeDtypeStruct((m, d), F32),
        scratch_shapes=[pltpu.VMEM((tt, d), F32)],
        compiler_params=_cparams(("parallel", "parallel", "arbitrary")),
        name="combine_final",
    )(idx.reshape(bsz, ne // eg, 1, eg * cap), ye, x1, gate.reshape(bsz, 1, d), norm_w.reshape(1, d))


def kernel(x, c, ctx, c_ctx, w_ada, b_ada, norm1, norm2, w_in, conv_qkv, a_log, dt_bias, gn_w, w_dn_out,
           conv_sc, w_sc_out, w_o, w_router, w_gate, w_up, w_down, norm_f):
    bsz, seq, d = x.shape
    m = bsz * seq
    l = 0

    pad = (-(bsz + 1)) % SUBLANES
    c_all = jnp.concatenate([c, c_ctx[None, :], jnp.zeros((pad, d), F32)], axis=0)
    mod_all = _ada(c_all, w_ada[l], b_ada[l])
    sh1, sc1, gt1, sh2, sc2, gt2 = jnp.split(mod_all[:bsz], 6, axis=-1)
    csh1, csc1 = mod_all[bsz:bsz + 1, :d], mod_all[bsz:bsz + 1, d:2 * d]

    wi = w_in[l]
    w_main = jnp.concatenate([wi[:, :N_QKV], wi[:, N_STATE_COLS:]], axis=1).astype(BF16)
    w_ba_t = jnp.pad(wi[:, N_QKV:N_STATE_COLS], ((0, 0), (0, LANES - N_BA))).astype(BF16).T
    lane_par = lambda a: jnp.pad(a.reshape(-1), (2 * DN_HEADS, LANES - N_BA))
    alog, dtb = lane_par(a_log[l]), lane_par(dt_bias[l])

    x2d = x.reshape(m, d)
    p_main, gbt_l = _in_proj(x2d, sh1, sc1, norm1[l], w_main, w_ba_t, alog, dtb, N_MAIN, seq, 1024, 1024)
    pc_qkv, gbt_c = _in_proj(ctx.reshape(bsz * CTX_LEN, d), csh1, csc1, norm1[l], w_main, w_ba_t, alog, dtb,
                             N_QKV, bsz * CTX_LEN, 1024, 1024)

    a_in = _deltanet(pc_qkv, gbt_c, p_main, gbt_l, conv_qkv[l], gn_w[l], bsz)

    w_r = jnp.pad(w_router[l], ((0, 0), (0, LANES - N_EXPERTS))).astype(BF16)
    mix = _mix(a_in, p_main, conv_sc[l], w_dn_out[l].astype(BF16), w_sc_out[l].astype(BF16), 1024, 512)
    x1, h2, logits = _oproj(mix, w_o[l].astype(BF16), x2d, gt1, sh2, sc2, norm2[l], w_r, seq, 512)

    cap = CAPACITY_FACTOR * seq // N_EXPERTS
    aff = jax.nn.softmax(logits[:, :N_EXPERTS].reshape(bsz, seq, N_EXPERTS), axis=-1)
    top_aff, idx = lax.top_k(aff.transpose(0, 2, 1), cap)
    flat_idx = (idx + (jnp.arange(bsz, dtype=idx.dtype) * seq)[:, None, None]).transpose(1, 0, 2)
    flat_idx = flat_idx.reshape(N_EXPERTS, bsz * cap)
    xs = h2.at[flat_idx].get(mode="promise_in_bounds")
    aff_e = top_aff.transpose(1, 0, 2).reshape(N_EXPERTS, bsz * cap, 1)
    ye = _moe_ffn(xs, w_gate[l], w_up[l], w_down[l], aff_e, 1024, 512, 512)
    out = _combine(idx, ye, x1, gt2, norm_f, seq, 512, 8)
    return out.reshape(bsz, seq, d)
```

```python
import functools

import jax
import jax.numpy as jnp
from jax import lax
from jax.experimental import pallas as pl
from jax.experimental.pallas import tpu as pltpu

F32 = jnp.float32
BF16 = jnp.bfloat16

D_MODEL = 2048
CTX_LEN = 256
GRID_W = 64
DN_HEADS = 8
DN_HEAD_DIM = 128
DN_WIDTH = DN_HEADS * DN_HEAD_DIM
SC_WIDTH = 1024
N_EXPERTS = 16
EXPERT_FF = 2048
CAPACITY_FACTOR = 2
NORM_EPS = 1e-6
N_QKV = 3 * DN_WIDTH
N_BA = 4 * DN_HEADS
N_STATE_COLS = N_QKV + N_BA
LANES = 128
SUBLANES = 8
COL_Z = N_QKV
COL_SC = COL_Z + DN_WIDTH
COL_GA = COL_SC + 3 * SC_WIDTH
COL_GB = COL_GA + D_MODEL
N_MAIN = COL_GB + D_MODEL
DN_BLOCK = LANES
DN_INVERSE_LEVELS = 7
DN_PREP_GROUP = 8
DN_HEADS_PER_STEP = 2

VMEM_LIMIT = 56 * 1024 * 1024


def _cparams(sem):
    return pltpu.CompilerParams(dimension_semantics=sem, vmem_limit_bytes=VMEM_LIMIT)


def _dot(a, b):
    return jnp.dot(a, b, preferred_element_type=F32)


def _dot_nt(a, b):
    return lax.dot_general(a, b, (((1,), (1,)), ((), ())), preferred_element_type=F32)


def _sigmoid(x):
    return 0.5 * jnp.tanh(0.5 * x) + 0.5


def _softplus(x):
    return jnp.maximum(x, 0.0) + jnp.log(1.0 + jnp.exp(-jnp.abs(x)))


def _ada_kernel(c_ref, w_ref, b_ref, o_ref):
    c = c_ref[...]
    s = (c * _sigmoid(c)).astype(BF16)
    o_ref[...] = _dot(s, w_ref[...].astype(BF16)) + b_ref[...]


def _ada(c_all, w, b):
    rows, d = c_all.shape
    n = w.shape[1]
    tn = 1024
    return pl.pallas_call(
        _ada_kernel,
        grid=(n // tn,),
        in_specs=[pl.BlockSpec((rows, d), lambda j: (0, 0)),
                  pl.BlockSpec((d, tn), lambda j: (0, j)),
                  pl.BlockSpec((1, tn), lambda j: (0, j))],
        out_specs=pl.BlockSpec((rows, tn), lambda j: (0, j)),
        out_shape=jax.ShapeDtypeStruct((rows, n), F32),
        compiler_params=_cparams(("arbitrary",)),
        name="ada_ln",
    )(c_all, w, b.reshape(1, n))


def _gate_act(raw, idx, alog, dtb):
    return jnp.where(idx < 2 * DN_HEADS, _sigmoid(raw), -jnp.exp(alog) * _softplus(raw + dtb))


def _in_proj_kernel(x_ref, sh_ref, sc_ref, nw_ref, w_ref, wbat_ref, alogt_ref, dtbt_ref, o_ref, gbt_ref, h_scr):
    @pl.when(pl.program_id(1) == 0)
    def _():
        x = x_ref[...]
        y = x * lax.rsqrt(jnp.mean(x * x, axis=-1, keepdims=True) + NORM_EPS) * nw_ref[...]
        hb = (y * (1.0 + sc_ref[0]) + sh_ref[0]).astype(BF16)
        h_scr[...] = hb
        raw_t = _dot_nt(wbat_ref[...], hb)
        gbt_ref[...] = _gate_act(raw_t, lax.broadcasted_iota(jnp.int32, raw_t.shape, 0), alogt_ref[...],
                                 dtbt_ref[...])

    o_ref[...] = _dot(h_scr[...], w_ref[...]).astype(o_ref.dtype)


def _in_proj(x2d, shift, scale, norm_w, w_main, w_ba_t, alog, dtb, n_out, rows_per_mod, tm, tn):
    m, d = x2d.shape
    nb = shift.shape[0]
    mod_idx = lambda i, j: ((i * tm) // rows_per_mod, 0, 0)
    const = lambda i, j: (0, 0)
    return pl.pallas_call(
        _in_proj_kernel,
        grid=(m // tm, n_out // tn),
        in_specs=[pl.BlockSpec((tm, d), lambda i, j: (i, 0)),
                  pl.BlockSpec((1, 1, d), mod_idx),
                  pl.BlockSpec((1, 1, d), mod_idx),
                  pl.BlockSpec((1, d), const),
                  pl.BlockSpec((d, tn), lambda i, j: (0, j)),
                  pl.BlockSpec((LANES, d), const),
                  pl.BlockSpec((LANES, 1), const),
                  pl.BlockSpec((LANES, 1), const)],
        out_specs=[pl.BlockSpec((tm, tn), lambda i, j: (i, j)),
                   pl.BlockSpec((LANES, tm), lambda i, j: (0, i))],
        out_shape=[jax.ShapeDtypeStruct((m, n_out), BF16),
                   jax.ShapeDtypeStruct((LANES, m), F32)],
        scratch_shapes=[pltpu.VMEM((tm, d), BF16)],
        compiler_params=_cparams(("parallel", "arbitrary")),
        name="in_proj",
    )(x2d, shift.reshape(nb, 1, d), scale.reshape(nb, 1, d), norm_w.reshape(1, d), w_main, w_ba_t,
      alog.reshape(LANES, 1), dtb.reshape(LANES, 1))


def _conv_silu(x, w, t):
    row = lax.broadcasted_iota(jnp.int32, x.shape, 0)
    prev = jnp.where(row == 0, 0.0, pltpu.roll(x, 1, 0))
    nxt = jnp.where(row == t - 1, 0.0, pltpu.roll(x, t - 1, 0))
    y = w[0:1, :] * prev + w[1:2, :] * x + w[2:3, :] * nxt
    return y * _sigmoid(y)


def _chunk_cumsum(x, axis, reverse):
    n = x.shape[axis]
    pos = lax.broadcasted_iota(jnp.int32, x.shape, axis) % DN_BLOCK
    s = 1
    while s < DN_BLOCK:
        if reverse:
            x = x + jnp.where(pos + s < DN_BLOCK, pltpu.roll(x, n - s, axis), 0.0)
        else:
            x = x + jnp.where(pos >= s, pltpu.roll(x, s, axis), 0.0)
        s *= 2
    return x


def _dn_load(t, h, q_ref, k_ref, v_ref, gbt_ref, cwq_ref, cwk_ref, cwv_ref, q_s, k_s, v_s, bfrow_s, bbrow_s, gfrow_s,
             grrow_s):
    q = _conv_silu(q_ref[...].astype(F32), cwq_ref[...], t)
    q_s[0:t, :] = q * lax.rsqrt(jnp.sum(q * q, axis=-1, keepdims=True) + NORM_EPS) * (DN_HEAD_DIM ** -0.5)
    k = _conv_silu(k_ref[...].astype(F32), cwk_ref[...], t)
    k_s[0:t, :] = k * lax.rsqrt(jnp.sum(k * k, axis=-1, keepdims=True) + NORM_EPS)
    v_s[0:t, :] = _conv_silu(v_ref[...].astype(F32), cwv_ref[...], t)
    for dst, base, reverse in ((bfrow_s, 0, None), (bbrow_s, DN_HEADS, None), (gfrow_s, 2 * DN_HEADS, False),
                               (grrow_s, 3 * DN_HEADS, True)):
        grp = gbt_ref[base:base + DN_HEADS, :]
        sub = lax.broadcasted_iota(jnp.int32, grp.shape, 0)
        val = jnp.broadcast_to(jnp.sum(jnp.where(sub == h, grp, 0.0), axis=0, keepdims=True), grp.shape)
        dst[:, 0:t] = val if reverse is None else _chunk_cumsum(val, 1, reverse)


def _dn_group_prep(chunks, need_o, q_s, k_s, v_s, bfrow_s, bbrow_s, gfrow_s, grrow_s, fwd_s, bwd_s):
    L = DN_BLOCK
    ri = lax.broadcasted_iota(jnp.int32, (L, L), 0)
    ci = lax.broadcasted_iota(jnp.int32, (L, L), 1)
    blk = ri ^ ci
    eye = jnp.where(ri == ci, 1.0, 0.0)
    chains = []
    for c in chunks:
        r0 = pl.multiple_of(c * L, L)
        rows = pl.ds(r0, L)
        q, k, v = q_s[rows, :], k_s[rows, :], v_s[rows, :]
        k16 = k.astype(BF16)
        kk = _dot_nt(k16, k16)
        qk = _dot_nt(q.astype(BF16), k16) if need_o else None
        for reverse, brow_s, grow_s, outs in ((False, bfrow_s, gfrow_s, fwd_s), (True, bbrow_s, grrow_s, bwd_s)):
            gc_row = jnp.broadcast_to(grow_s[0:1, pl.ds(r0, L)], (L, L))
            gc_col = jnp.transpose(gc_row)
            beta = jnp.transpose(jnp.broadcast_to(brow_s[0:1, pl.ds(r0, L)], (L, L)))
            incl, strict = (ri <= ci, ri < ci) if reverse else (ri >= ci, ri > ci)
            decay = jnp.exp(jnp.where(incl, gc_col - gc_row, -jnp.inf))
            a_tri = jnp.where(strict, beta * kk * decay, 0.0)
            g_last = gc_col[0:1, :] if reverse else gc_col[L - 1:L, :]
            chains.append(dict(c=c, rows=rows, q=q, k=k, v=v, qk=qk, beta=beta, gc_col=gc_col, decay=decay,
                               a_tri=a_tri, g_last=g_last, t_inv=eye, outs=outs))
    for ch in chains:
        ch["t_inv"] = eye - jnp.where(blk == 1, ch["a_tri"], 0.0)
    for level in range(1, DN_INVERSE_LEVELS):
        for ch in chains:
            ch["t16"] = ch["t_inv"].astype(BF16)
            a_off = jnp.where((blk >> level) == 1, ch["a_tri"], 0.0).astype(BF16)
            ch["m"] = _dot(a_off, ch["t16"]).astype(BF16)
        for ch in chains:
            ch["t_inv"] = ch["t_inv"] - _dot(ch["t16"], ch["m"])
    for ch in chains:
        egc = jnp.exp(ch["gc_col"])
        rhs = jnp.concatenate([ch["v"] * ch["beta"], ch["k"] * ch["beta"] * egc], axis=-1).astype(BF16)
        ch["sol"] = _dot(ch["t_inv"].astype(BF16), rhs)
        ch["egc"] = egc
    for ch in chains:
        u_s, w_s, qk_s, qd_s, kdt_s, egl_s = ch["outs"]
        rows, c = ch["rows"], ch["c"]
        u_s[rows, :] = ch["sol"][:, :DN_HEAD_DIM]
        w_s[rows, :] = ch["sol"][:, DN_HEAD_DIM:].astype(BF16)
        kd = ch["k"] * jnp.exp(ch["g_last"] - ch["gc_col"])
        kdt_s[rows, :] = jnp.transpose(kd).astype(BF16)
        egl_s[pl.ds(pl.multiple_of(c * SUBLANES, SUBLANES), SUBLANES), :] = jnp.broadcast_to(
            jnp.exp(ch["g_last"]), (SUBLANES, LANES))
        if need_o:
            qk_s[rows, :] = (ch["qk"] * ch["decay"]).astype(BF16)
            qd_s[rows, :] = (ch["q"] * ch["egc"]).astype(BF16)


def _dn_scan(n, need_o, states, heads):
    group = min(n, DN_PREP_GROUP)
    for vec_s, fwd_s, bwd_s, _ in heads:
        def prep(i, carry, vec_s=vec_s, fwd_s=fwd_s, bwd_s=bwd_s):
            _dn_group_prep([i * group + j for j in range(group)], need_o, *vec_s, fwd_s, bwd_s)
            return carry

        lax.fori_loop(0, n // group, prep, 0)

    def step(c, carry):
        chains = []
        for (s_f, s_b), (_, fwd_s, bwd_s, o_s) in zip(carry, heads):
            chains.append(dict(s=s_f, c=c, sc=fwd_s, o_s=o_s))
            chains.append(dict(s=s_b, c=n - 1 - c, sc=bwd_s, o_s=o_s))
        for ch in chains:
            u_s, w_s, qk_s, qd_s, kdt_s, egl_s = ch["sc"]
            ch["rows"] = pl.ds(pl.multiple_of(ch["c"] * DN_BLOCK, DN_BLOCK), DN_BLOCK)
            s16 = ch["s"].astype(BF16)
            ch["ws"] = _dot(w_s[ch["rows"], :], s16)
            if need_o:
                ch["qs"] = _dot(qd_s[ch["rows"], :], s16)
        for ch in chains:
            u_s, w_s, qk_s, qd_s, kdt_s, egl_s = ch["sc"]
            v_new = (u_s[ch["rows"], :] - ch["ws"]).astype(BF16)
            ch["kv"] = _dot(kdt_s[ch["rows"], :], v_new)
            if need_o:
                ch["qv"] = _dot(qk_s[ch["rows"], :], v_new)
        out = []
        for ch in chains:
            egl_s = ch["sc"][5]
            egl = egl_s[pl.ds(pl.multiple_of(ch["c"] * SUBLANES, SUBLANES), 1), :]
            out.append(ch["s"] * egl + ch["kv"])
            if need_o:
                ch["o_s"][ch["rows"], :] += ch["qs"] + ch["qv"]
        return tuple((out[2 * i], out[2 * i + 1]) for i in range(len(heads)))

    return lax.fori_loop(0, n, step, tuple(states))


def _deltanet_kernel(qc_ref, kc_ref, vc_ref, gbtc_ref, ql_ref, kl_ref, vl_ref, zl_ref, gbtl_ref,
                     cwq_ref, cwk_ref, cwv_ref, gnw_ref, out_ref, *scratch):
    hd = DN_HEAD_DIM
    n_heads = out_ref.shape[1] // hd
    per_head = len(scratch) // n_heads
    t_ctx, t_lat = qc_ref.shape[0], ql_ref.shape[0]
    heads = []
    for i in range(n_heads):
        sc = scratch[i * per_head:(i + 1) * per_head]
        heads.append((sc[0:7], sc[7:13], sc[13:19], sc[19]))
    lanes = lambda ref, i: ref.at[:, pl.ds(i * hd, hd)]
    zero = jnp.zeros((hd, hd), F32)

    for i, (vec_s, _, _, _) in enumerate(heads):
        _dn_load(t_ctx, pl.program_id(1) * n_heads + i, lanes(qc_ref, i), lanes(kc_ref, i), lanes(vc_ref, i), gbtc_ref,
                 lanes(cwq_ref, i), lanes(cwk_ref, i), lanes(cwv_ref, i), *vec_s)
    states = _dn_scan(t_ctx // DN_BLOCK, False, [(zero, zero)] * n_heads, heads)

    for i, (vec_s, _, _, o_s) in enumerate(heads):
        _dn_load(t_lat, pl.program_id(1) * n_heads + i, lanes(ql_ref, i), lanes(kl_ref, i), lanes(vl_ref, i), gbtl_ref,
                 lanes(cwq_ref, i), lanes(cwk_ref, i), lanes(cwv_ref, i), *vec_s)
        o_s[...] = jnp.zeros_like(o_s)
    _dn_scan(t_lat // DN_BLOCK, True, states, heads)

    for i, (_, _, _, o_s) in enumerate(heads):
        o = o_s[...]
        z = zl_ref[:, i * hd:(i + 1) * hd].astype(F32)
        o = o * lax.rsqrt(jnp.mean(o * o, axis=-1, keepdims=True) + NORM_EPS) * gnw_ref[...]
        out_ref[:, i * hd:(i + 1) * hd] = (o * (z * _sigmoid(z))).astype(out_ref.dtype)


def _deltanet(pc_qkv, gbt_c, p_main, gbt_l, conv_w, gn_w, bsz):
    t_ctx = pc_qkv.shape[0] // bsz
    t_lat = p_main.shape[0] // bsz
    hd, nh, hps = DN_HEAD_DIM, DN_HEADS, DN_HEADS_PER_STEP
    wide = hps * hd
    col = lambda off: (lambda b, h: (b, off // hps + h))
    col0 = lambda b, h: (0, b)
    wcol = lambda off: (lambda b, h: (0, off // hps + h))
    vec = lambda dt: pltpu.VMEM((t_lat, hd), dt)
    rowvec = pltpu.VMEM((SUBLANES, t_lat), F32)
    per_dir = [vec(F32), vec(BF16), vec(BF16), vec(BF16), vec(BF16),
               pltpu.VMEM((t_lat // DN_BLOCK * SUBLANES, LANES), F32)]
    per_head = [vec(F32)] * 3 + [rowvec] * 4 + per_dir + per_dir + [vec(F32)]
    return pl.pallas_call(
        _deltanet_kernel,
        grid=(bsz, nh // hps),
        in_specs=[pl.BlockSpec((t_ctx, wide), col(0)), pl.BlockSpec((t_ctx, wide), col(nh)),
                  pl.BlockSpec((t_ctx, wide), col(2 * nh)), pl.BlockSpec((LANES, t_ctx), col0),
                  pl.BlockSpec((t_lat, wide), col(0)), pl.BlockSpec((t_lat, wide), col(nh)),
                  pl.BlockSpec((t_lat, wide), col(2 * nh)), pl.BlockSpec((t_lat, wide), col(3 * nh)),
                  pl.BlockSpec((LANES, t_lat), col0),
                  pl.BlockSpec((3, wide), wcol(0)), pl.BlockSpec((3, wide), wcol(nh)),
                  pl.BlockSpec((3, wide), wcol(2 * nh)),
                  pl.BlockSpec((1, hd), lambda b, h: (0, 0))],
        out_specs=pl.BlockSpec((t_lat, wide), lambda b, h: (b, h)),
        out_shape=jax.ShapeDtypeStruct((p_main.shape[0], DN_WIDTH), BF16),
        scratch_shapes=per_head * hps,
        compiler_params=_cparams(("parallel", "arbitrary")),
        name="deltanet",
    )(pc_qkv, pc_qkv, pc_qkv, gbt_c, p_main, p_main, p_main, p_main, gbt_l,
      conv_w, conv_w, conv_w, gn_w.reshape(1, hd))


def _mix_kernel(a_ref, bg_ref, cg_ref, xin_ref, cw_ref, ga_ref, gb_ref, wdn_ref, wsc_ref, o_ref, b_scr):
    @pl.when(pl.program_id(1) == 0)
    def _():
        zc = cg_ref[...].astype(F32) * xin_ref[...].astype(F32)
        tm = zc.shape[0]
        pos = lax.broadcasted_iota(jnp.int32, zc.shape, 0) % GRID_W
        prev = jnp.where(pos == 0, 0.0, pltpu.roll(zc, 1, 0))
        nxt = jnp.where(pos == GRID_W - 1, 0.0, pltpu.roll(zc, tm - 1, 0))
        cw = cw_ref[...]
        conv = cw[0:1, :] * prev + cw[1:2, :] * zc + cw[2:3, :] * nxt
        b_scr[...] = (bg_ref[...].astype(F32) * conv).astype(BF16)

    ya = _dot(a_ref[...], wdn_ref[...])
    yb = _dot(b_scr[...], wsc_ref[...])
    mix = _sigmoid(ga_ref[...].astype(F32)) * ya + _sigmoid(gb_ref[...].astype(F32)) * yb
    o_ref[...] = mix.astype(o_ref.dtype)


def _mix(a_in, p_main, conv_sc, w_dn, w_sc, tm, tj):
    m = a_in.shape[0]
    d = w_dn.shape[1]
    ga0, gb0 = COL_GA // tj, COL_GB // tj
    sc0 = COL_SC // SC_WIDTH
    const = lambda i, j: (0, 0)
    rowblk = lambda i, j: (i, 0)
    return pl.pallas_call(
        _mix_kernel,
        grid=(m // tm, d // tj),
        in_specs=[pl.BlockSpec((tm, DN_WIDTH), rowblk),
                  pl.BlockSpec((tm, SC_WIDTH), lambda i, j: (i, sc0)),
                  pl.BlockSpec((tm, SC_WIDTH), lambda i, j: (i, sc0 + 1)),
                  pl.BlockSpec((tm, SC_WIDTH), lambda i, j: (i, sc0 + 2)),
                  pl.BlockSpec((3, SC_WIDTH), const),
                  pl.BlockSpec((tm, tj), lambda i, j: (i, ga0 + j)),
                  pl.BlockSpec((tm, tj), lambda i, j: (i, gb0 + j)),
                  pl.BlockSpec((DN_WIDTH, tj), lambda i, j: (0, j)),
                  pl.BlockSpec((SC_WIDTH, tj), lambda i, j: (0, j))],
        out_specs=pl.BlockSpec((tm, tj), lambda i, j: (i, j)),
        out_shape=jax.ShapeDtypeStruct((m, d), BF16),
        scratch_shapes=[pltpu.VMEM((tm, SC_WIDTH), BF16)],
        compiler_params=_cparams(("parallel", "arbitrary")),
        name="branch_mix",
    )(a_in, p_main, p_main, p_main, conv_sc, p_main, p_main, w_dn, w_sc)


def _oproj_kernel(mix_ref, wo_ref, x_ref, gt_ref, sh_ref, sc_ref, nw_ref, wr_ref, x1_ref, h2_ref, lg_ref):
    x1 = x_ref[...] + gt_ref[0] * _dot(mix_ref[...], wo_ref[...])
    x1_ref[...] = x1
    y = x1 * lax.rsqrt(jnp.mean(x1 * x1, axis=-1, keepdims=True) + NORM_EPS) * nw_ref[...]
    hb = (y * (1.0 + sc_ref[0]) + sh_ref[0]).astype(BF16)
    h2_ref[...] = hb
    lg_ref[...] = _dot(hb, wr_ref[...])


def _oproj(mix, w_o, x2d, gate, shift, scale, norm_w, w_r, rows_per_mod, tm):
    m, d = x2d.shape
    nb = gate.shape[0]
    mod_idx = lambda i: ((i * tm) // rows_per_mod, 0, 0)
    const = lambda i: (0, 0)
    rowblk = lambda i: (i, 0)
    resident = pl.Buffered(1)
    return pl.pallas_call(
        _oproj_kernel,
        grid=(m // tm,),
        in_specs=[pl.BlockSpec((tm, d), rowblk),
                  pl.BlockSpec((d, d), const, pipeline_mode=resident),
                  pl.BlockSpec((tm, d), rowblk),
                  pl.BlockSpec((1, 1, d), mod_idx),
                  pl.BlockSpec((1, 1, d), mod_idx),
                  pl.BlockSpec((1, 1, d), mod_idx),
                  pl.BlockSpec((1, d), const),
                  pl.BlockSpec((d, LANES), const, pipeline_mode=resident)],
        out_specs=[pl.BlockSpec((tm, d), rowblk),
                   pl.BlockSpec((tm, d), rowblk),
                   pl.BlockSpec((tm, LANES), rowblk)],
        out_shape=[jax.ShapeDtypeStruct((m, d), F32),
                   jax.ShapeDtypeStruct((m, d), BF16),
                   jax.ShapeDtypeStruct((m, LANES), F32)],
        compiler_params=_cparams(("parallel",)),
        name="out_proj",
    )(mix, w_o, x2d, gate.reshape(nb, 1, d), shift.reshape(nb, 1, d), scale.reshape(nb, 1, d),
      norm_w.reshape(1, d), w_r)


def _moe_kernel(xs_ref, wg_ref, wu_ref, wd_ref, aff_ref, o_ref, hid_scr, *, nf, tf):
    s = pl.program_id(2)

    @pl.when(s < nf)
    def _():
        x = xs_ref[0]
        g = _dot(x, wg_ref[0].astype(BF16))
        u = _dot(x, wu_ref[0].astype(BF16))
        col = pl.multiple_of(s * tf, tf)
        hid_scr[:, pl.ds(col, tf)] = (g * _sigmoid(g) * u).astype(BF16)

    @pl.when(s >= nf)
    def _():
        y = _dot(hid_scr[...], wd_ref[0].astype(BF16))
        o_ref[0] = (y * aff_ref[0]).astype(o_ref.dtype)


def _moe_ffn(xs, w_gate, w_up, w_down, aff, tm, tf, tn):
    e, m, d = xs.shape
    ff = w_gate.shape[2]
    nf, nn = ff // tf, d // tn
    up_idx = lambda e_, i, s: (e_, 0, jnp.minimum(s, nf - 1))
    down_idx = lambda e_, i, s: (e_, 0, jnp.maximum(s - nf, 0))
    return pl.pallas_call(
        functools.partial(_moe_kernel, nf=nf, tf=tf),
        grid=(e, m // tm, nf + nn),
        in_specs=[pl.BlockSpec((1, tm, d), lambda e_, i, s: (e_, i, 0)),
                  pl.BlockSpec((1, d, tf), up_idx),
                  pl.BlockSpec((1, d, tf), up_idx),
                  pl.BlockSpec((1, ff, tn), down_idx),
                  pl.BlockSpec((1, tm, 1), lambda e_, i, s: (e_, i, 0))],
        out_specs=pl.BlockSpec((1, tm, tn), lambda e_, i, s: (e_, i, jnp.maximum(s - nf, 0))),
        out_shape=jax.ShapeDtypeStruct((e, m, d), BF16),
        scratch_shapes=[pltpu.VMEM((tm, ff), BF16)],
        compiler_params=_cparams(("parallel", "parallel", "arbitrary")),
        name="moe_ffn",
    )(xs, w_gate, w_up, w_down, aff)


def _combine_kernel(idx_ref, ye_ref, x_ref, gt_ref, nw_ref, o_ref, acc):
    g = pl.program_id(2)
    tt, d = acc.shape
    idx = idx_ref[0, 0]
    tok = lax.broadcasted_iota(jnp.int32, (tt, idx.shape[-1]), 0) + pl.program_id(1) * tt
    onehot = jnp.where(tok == idx, 1.0, 0.0).astype(BF16)
    contrib = _dot(onehot, ye_ref[...].reshape(idx.shape[-1], d))

    @pl.when(g == 0)
    def _():
        acc[...] = contrib

    @pl.when(g > 0)
    def _():
        acc[...] += contrib

    @pl.when(g == pl.num_programs(2) - 1)
    def _():
        x2 = x_ref[...] + gt_ref[0] * acc[...]
        o_ref[...] = x2 * lax.rsqrt(jnp.mean(x2 * x2, axis=-1, keepdims=True) + NORM_EPS) * nw_ref[...]


def _combine(idx, ye, x1, gate, norm_w, seq, tt, eg):
    bsz, ne, cap = idx.shape
    m, d = x1.shape
    tiles = seq // tt
    return pl.pallas_call(
        _combine_kernel,
        grid=(bsz, tiles, ne // eg),
        in_specs=[pl.BlockSpec((1, 1, 1, eg * cap), lambda b, t, g: (b, g, 0, 0)),
                  pl.BlockSpec((eg, cap, d), lambda b, t, g: (g, b, 0)),
                  pl.BlockSpec((tt, d), lambda b, t, g: (b * tiles + t, 0)),
                  pl.BlockSpec((1, 1, d), lambda b, t, g: (b, 0, 0)),
                  pl.BlockSpec((1, d), lambda b, t, g: (0, 0))],
        out_specs=pl.BlockSpec((tt, d), lambda b, t, g: (b * tiles + t, 0)),
        out_shape=jax.ShapeDtypeStruct((m, d), F32),
        scratch_shapes=[pltpu.VMEM((tt, d), F32)],
        compiler_params=_cparams(("parallel", "parallel", "arbitrary")),
        name="combine_final",
    )(idx.reshape(bsz, ne // eg, 1, eg * cap), ye, x1, gate.reshape(bsz, 1, d), norm_w.reshape(1, d))


def kernel(x, c, ctx, c_ctx, w_ada, b_ada, norm1, norm2, w_in, conv_qkv, a_log, dt_bias, gn_w, w_dn_out,
           conv_sc, w_sc_out, w_o, w_router, w_gate, w_up, w_down, norm_f):
    bsz, seq, d = x.shape
    m = bsz * seq
    l = 0

    pad = (-(bsz + 1)) % SUBLANES
    c_all = jnp.concatenate([c, c_ctx[None, :], jnp.zeros((pad, d), F32)], axis=0)
    mod_all = _ada(c_all, w_ada[l], b_ada[l])
    sh1, sc1, gt1, sh2, sc2, gt2 = jnp.split(mod_all[:bsz], 6, axis=-1)
    csh1, csc1 = mod_all[bsz:bsz + 1, :d], mod_all[bsz:bsz + 1, d:2 * d]

    wi = w_in[l]
    w_main = jnp.concatenate([wi[:, :N_QKV], wi[:, N_STATE_COLS:]], axis=1).astype(BF16)
    w_ba_t = jnp.pad(wi[:, N_QKV:N_STATE_COLS], ((0, 0), (0, LANES - N_BA))).astype(BF16).T
    lane_par = lambda a: jnp.pad(a.reshape(-1), (2 * DN_HEADS, LANES - N_BA))
    alog, dtb = lane_par(a_log[l]), lane_par(dt_bias[l])

    x2d = x.reshape(m, d)
    p_main, gbt_l = _in_proj(x2d, sh1, sc1, norm1[l], w_main, w_ba_t, alog, dtb, N_MAIN, seq, 1024, 1024)
    pc_qkv, gbt_c = _in_proj(ctx.reshape(bsz * CTX_LEN, d), csh1, csc1, norm1[l], w_main, w_ba_t, alog, dtb,
                             N_QKV, bsz * CTX_LEN, 1024, 1024)

    a_in = _deltanet(pc_qkv, gbt_c, p_main, gbt_l, conv_qkv[l], gn_w[l], bsz)

    w_r = jnp.pad(w_router[l], ((0, 0), (0, LANES - N_EXPERTS))).astype(BF16)
    mix = _mix(a_in, p_main, conv_sc[l], w_dn_out[l].astype(BF16), w_sc_out[l].astype(BF16), 1024, 512)
    x1, h2, logits = _oproj(mix, w_o[l].astype(BF16), x2d, gt1, sh2, sc2, norm2[l], w_r, seq, 512)

    cap = CAPACITY_FACTOR * seq // N_EXPERTS
    aff = jax.nn.softmax(logits[:, :N_EXPERTS].reshape(bsz, seq, N_EXPERTS), axis=-1)
    top_aff, idx = lax.top_k(aff.transpose(0, 2, 1), cap)
    flat_idx = (idx + (jnp.arange(bsz, dtype=idx.dtype) * seq)[:, None, None]).transpose(1, 0, 2)
    flat_idx = flat_idx.reshape(N_EXPERTS, bsz * cap)
    xs = h2.at[flat_idx].get(mode="promise_in_bounds")
    aff_e = top_aff.transpose(1, 0, 2).reshape(N_EXPERTS, bsz * cap, 1)
    ye = _moe_ffn(xs, w_gate[l], w_up[l], w_down[l], aff_e, 2048, 256, 256)
    out = _combine(idx, ye, x1, gt2, norm_f, seq, 512, 8)
    return out.reshape(bsz, seq, d)
```
